```python
import math
import jax, jax.numpy as jnp
from jax import lax
import numpy as np

D_MODEL = 1024
BATCH = 8
SEQ = 2048
DEPTH = 2

LRU_WIDTH = 512
LRU_BLOCKS = 8
LRU_BLOCK = LRU_WIDTH // LRU_BLOCKS
LRU_CONV = 4
LRU_C = 8.0
S5_WIDTH = 512
S5_GROUP = 16
S5_GROUPS = S5_WIDTH // S5_GROUP
S5_STATE = 64
RET_HEADS = 4
RET_HEAD_DIM = 128
RET_WIDTH = RET_HEADS * RET_HEAD_DIM
RET_CHUNK = 128
ROPE_BASE = 10000.0
MIX_WIDTH = LRU_WIDTH + S5_WIDTH + RET_WIDTH
IN_WIDTH = 2 * LRU_WIDTH + S5_WIDTH + 4 * RET_WIDTH
IN_SPLITS = (LRU_WIDTH, 2 * LRU_WIDTH, 2 * LRU_WIDTH + S5_WIDTH,
             2 * LRU_WIDTH + S5_WIDTH + RET_WIDTH,
             2 * LRU_WIDTH + S5_WIDTH + 2 * RET_WIDTH,
             2 * LRU_WIDTH + S5_WIDTH + 3 * RET_WIDTH)
D_FF = 3 * D_MODEL
FFN_CONV = 3
NORM_EPS = 1e-6

kernel_name = 'hymba_style_lru_s5_retention_convffn'


def rmsnorm(x, gain):
    xf = x.astype(jnp.float32)
    y = xf * lax.rsqrt(jnp.mean(xf * xf, axis=-1, keepdims=True) + NORM_EPS)
    return (y * gain.astype(jnp.float32)).astype(x.dtype)


def causal_dwconv(x, w, b):
    k, c = w.shape
    y = lax.conv_general_dilated(x, w[:, None, :].astype(x.dtype), window_strides=(1,),
                                 padding=[(k - 1, 0)],
                                 dimension_numbers=('NWC', 'WIO', 'NWC'),
                                 feature_group_count=c)
    return y + b.astype(x.dtype)


def _lin_combine(e1, e2):
    a1, b1 = e1
    a2, b2 = e2
    return a1 * a2, a2 * b1 + b2


def _clin_combine(e1, e2):
    ar1, ai1, br1, bi1 = e1
    ar2, ai2, br2, bi2 = e2
    return (ar1 * ar2 - ai1 * ai2,
            ar1 * ai2 + ai1 * ar2,
            ar2 * br1 - ai2 * bi1 + br2,
            ar2 * bi1 + ai2 * br1 + bi2)


def rglru_mixer(xb, yb, conv_w, conv_b, wa, ba, wx, bx, lam):
    bsz, L, _ = xb.shape
    xc = causal_dwconv(xb, conv_w, conv_b)
    xh = xc.reshape(bsz, L, LRU_BLOCKS, LRU_BLOCK)
    r = jax.nn.sigmoid(jnp.einsum('blhi,hij->blhj', xh, wa) + ba).reshape(bsz, L, LRU_WIDTH)
    i = jax.nn.sigmoid(jnp.einsum('blhi,hij->blhj', xh, wx) + bx).reshape(bsz, L, LRU_WIDTH)
    log_a = -LRU_C * r.astype(jnp.float32) * jax.nn.softplus(-lam.astype(jnp.float32))
    a = jnp.exp(log_a)
    b = jnp.sqrt(-jnp.expm1(2.0 * log_a)) * (i * xc).astype(jnp.float32)
    _, h = lax.associative_scan(_lin_combine, (a.transpose(1, 0, 2), b.transpose(1, 0, 2)), axis=0)
    return h.transpose(1, 0, 2).astype(xb.dtype) * jax.nn.gelu(yb)


def s5_mixer(u, lam_re, lam_im, log_dt, b_re, b_im, c_re, c_im, d, w_glu, b_glu):
    bsz, L, _ = u.shape
    f32 = jnp.float32
    uf = u.astype(f32).reshape(bsz, L, S5_GROUPS, S5_GROUP).transpose(1, 0, 2, 3)
    dt = jnp.exp(log_dt.astype(f32))[:, None]
    lr, li = lam_re.astype(f32), lam_im.astype(f32)
    mag = jnp.exp(lr * dt)
    abar_re, abar_im = mag * jnp.cos(li * dt), mag * jnp.sin(li * dt)
    den = lr * lr + li * li
    nr, ni = abar_re - 1.0, abar_im
    coef_re = ((nr * lr + ni * li) / den)[..., None]
    coef_im = ((ni * lr - nr * li) / den)[..., None]
    br, bi = b_re.astype(f32), b_im.astype(f32)
    bbar_re = coef_re * br - coef_im * bi
    bbar_im = coef_re * bi + coef_im * br
    bu_re = jnp.einsum('lbgc,gpc->lbgp', uf, bbar_re)
    bu_im = jnp.einsum('lbgc,gpc->lbgp', uf, bbar_im)
    shape_a = (L, 1, S5_GROUPS, S5_STATE)
    ar = jnp.broadcast_to(abar_re[None, None], shape_a)
    ai = jnp.broadcast_to(abar_im[None, None], shape_a)
    _, _, xr, xi = lax.associative_scan(_clin_combine, (ar, ai, bu_re, bu_im), axis=0)
    y = (jnp.einsum('lbgp,gcp->lbgc', xr, c_re.astype(f32))
         - jnp.einsum('lbgp,gcp->lbgc', xi, c_im.astype(f32))
         + d.astype(f32).reshape(S5_GROUPS, S5_GROUP) * uf)
    y = y.transpose(1, 0, 2, 3).reshape(bsz, L, S5_WIDTH)
    z = jax.nn.gelu(y)
    out = z * jax.nn.sigmoid(jnp.einsum('ble,ef->blf', z, w_glu.astype(f32)) + b_glu.astype(f32))
    return out.astype(u.dtype)


def rotary_tables(positions):
    half = RET_HEAD_DIM // 2
    inv = ROPE_BASE ** (-jnp.arange(half, dtype=jnp.float32) * 2.0 / RET_HEAD_DIM)
    ang = positions.astype(jnp.float32)[..., None] * inv
    return jnp.cos(ang)[:, None], jnp.sin(ang)[:, None]


def apply_rotary(t, cos, sin):
    half = RET_HEAD_DIM // 2
    t1, t2 = t[..., :half], t[..., half:]
    return jnp.concatenate([t1 * cos - t2 * sin, t1 * sin + t2 * cos], axis=-1)


def retention_mixer(q, k, v, g, cos, sin, gn_gain):
    bsz, L, _ = q.shape
    H, Dh, C = RET_HEADS, RET_HEAD_DIM, RET_CHUNK
    nc = L // C
    f32 = jnp.float32

    def heads(t):
        return t.astype(f32).reshape(bsz, L, H, Dh).transpose(0, 2, 1, 3)

    qh = apply_rotary(heads(q), cos, sin)
    kh = apply_rotary(heads(k), cos, sin) * (Dh ** -0.5)
    qc = qh.reshape(bsz, H, nc, C, Dh)
    kc = kh.reshape(bsz, H, nc, C, Dh)
    vc = heads(v).reshape(bsz, H, nc, C, Dh)
    log_gamma = jnp.log1p(-(2.0 ** (-5.0 - jnp.arange(H, dtype=f32))))
    idx = jnp.arange(C, dtype=f32)
    rel = idx[:, None] - idx[None, :]
    intra_decay = jnp.where(rel >= 0, jnp.exp(log_gamma[:, None, None] * jnp.maximum(rel, 0.0)), 0.0)
    scores = jnp.einsum('bhncd,bhnmd->bhncm', qc, kc) * intra_decay[None, :, None]
    intra = jnp.einsum('bhncm,bhnme->bhnce', scores, vc)
    k_decay = jnp.exp(log_gamma[:, None] * (C - 1.0 - idx))
    kv = jnp.einsum('bhnmd,bhnme->nbhde', kc * k_decay[None, :, None, :, None], vc)
    chunk_decay = jnp.exp(log_gamma * C)[None, :, None, None]

    def step(state, kv_n):
        return chunk_decay * state + kv_n, state

    _, prev = lax.scan(step, jnp.zeros((bsz, H, Dh, Dh), f32), kv)
    q_decay = jnp.exp(log_gamma[:, None] * (idx + 1.0))
    cross = jnp.einsum('bhncd,nbhde->bhnce', qc * q_decay[None, :, None, :, None], prev)
    o = (intra + cross).reshape(bsz, H, L, Dh)
    mu = jnp.mean(o, axis=-1, keepdims=True)
    var = jnp.mean(jnp.square(o - mu), axis=-1, keepdims=True)
    o = (o - mu) * lax.rsqrt(var + NORM_EPS)
    o = o.transpose(0, 2, 1, 3).reshape(bsz, L, RET_WIDTH) * gn_gain.astype(f32)
    return (o * jax.nn.silu(g.astype(f32))).astype(q.dtype)


def gated_conv_ffn(xn, w_up, conv_w, conv_b, w_down):
    up = causal_dwconv(jnp.einsum('bld,de->ble', xn, w_up), conv_w, conv_b)
    val, gate = jnp.split(up, 2, axis=-1)
    return jnp.einsum('blf,fd->bld', jax.nn.gelu(gate) * val, w_down)


def setup_inputs(seed: int = 0) -> dict:
    key = jax.random.key(seed)
    ks = iter(jax.random.split(key, 48))
    f32 = jnp.float32
    L = DEPTH
    G, P = S5_GROUPS, S5_STATE

    def nrm(shape, scale):
        return jax.random.normal(next(ks), shape, f32) * scale

    def gain(shape):
        return 1.0 + nrm(shape, 0.02)

    x = jax.random.normal(next(ks), (BATCH, SEQ, D_MODEL), f32)
    offset = jax.random.randint(next(ks), (BATCH, 1), 0, 4096, dtype=jnp.int32)
    positions = offset + jnp.arange(SEQ, dtype=jnp.int32)[None, :]
    u = jax.random.uniform(next(ks), (L, LRU_WIDTH), f32, 0.9, 0.999)
    a = u ** (1.0 / LRU_C)
    lru_lambda = jnp.log(a) - jnp.log1p(-a)
    s5_log_dt = jax.random.uniform(next(ks), (L, G), f32, math.log(1e-3), math.log(1e-1))
    return {
        'x': x,
        'positions': positions,
        'norm_mix': gain((L, D_MODEL)),
        'w_in': nrm((L, D_MODEL, IN_WIDTH), D_MODEL ** -0.5),
        'lru_conv_w': nrm((L, LRU_CONV, LRU_WIDTH), LRU_CONV ** -0.5),
        'lru_conv_b': nrm((L, LRU_WIDTH), 0.01),
        'lru_wa': nrm((L, LRU_BLOCKS, LRU_BLOCK, LRU_BLOCK), LRU_BLOCK ** -0.5),
        'lru_ba': nrm((L, LRU_BLOCKS, LRU_BLOCK), 0.01),
        'lru_wx': nrm((L, LRU_BLOCKS, LRU_BLOCK, LRU_BLOCK), LRU_BLOCK ** -0.5),
        'lru_bx': nrm((L, LRU_BLOCKS, LRU_BLOCK), 0.01),
        'lru_lambda': lru_lambda,
        'lru_norm': gain((L, LRU_WIDTH)),
        's5_lambda_re': -0.5 + nrm((L, G, P), 0.01),
        's5_lambda_im': jnp.pi * jnp.arange(P, dtype=f32) + nrm((L, G, P), 0.01),
        's5_log_dt': s5_log_dt,
        's5_b_re': nrm((L, G, P, S5_GROUP), S5_GROUP ** -0.5),
        's5_b_im': nrm((L, G, P, S5_GROUP), S5_GROUP ** -0.5),
        's5_c_re': nrm((L, G, S5_GROUP, P), P ** -0.5),
        's5_c_im': nrm((L, G, S5_GROUP, P), P ** -0.5),
        's5_d': nrm((L, S5_WIDTH), 0.5),
        's5_w_glu': nrm((L, S5_WIDTH, S5_WIDTH), S5_WIDTH ** -0.5),
        's5_b_glu': nrm((L, S5_WIDTH), 0.01),
        's5_norm': gain((L, S5_WIDTH)),
        'ret_norm': gain((L, RET_WIDTH)),
        'w_out': nrm((L, MIX_WIDTH, D_MODEL), MIX_WIDTH ** -0.5),
        'norm_ffn': gain((L, D_MODEL)),
        'w_up': nrm((L, D_MODEL, 2 * D_FF), D_MODEL ** -0.5),
        'ffn_conv_w': nrm((L, FFN_CONV, 2 * D_FF), FFN_CONV ** -0.5),
        'ffn_conv_b': nrm((L, 2 * D_FF), 0.01),
        'w_down': nrm((L, D_FF, D_MODEL), D_FF ** -0.5),
        'norm_final': gain((D_MODEL,)),
    }


def reference(x, positions, norm_mix, w_in, lru_conv_w, lru_conv_b, lru_wa, lru_ba, lru_wx, lru_bx,
              lru_lambda, lru_norm, s5_lambda_re, s5_lambda_im, s5_log_dt, s5_b_re, s5_b_im,
              s5_c_re, s5_c_im, s5_d, s5_w_glu, s5_b_glu, s5_norm, ret_norm, w_out, norm_ffn,
              w_up, ffn_conv_w, ffn_conv_b, w_down, norm_final):
    cos, sin = rotary_tables(positions)
    h = x
    for l in range(DEPTH):
        xn = rmsnorm(h, norm_mix[l])
        proj = jnp.einsum('bld,de->ble', xn, w_in[l])
        lru_x, lru_gate, s5_u, q, k, v, g = jnp.split(proj, IN_SPLITS, axis=-1)
        y_lru = rmsnorm(rglru_mixer(lru_x, lru_gate, lru_conv_w[l], lru_conv_b[l], lru_wa[l],
                                    lru_ba[l], lru_wx[l], lru_bx[l], lru_lambda[l]), lru_norm[l])
        y_s5 = rmsnorm(s5_mixer(s5_u, s5_lambda_re[l], s5_lambda_im[l], s5_log_dt[l], s5_b_re[l],
                                s5_b_im[l], s5_c_re[l], s5_c_im[l], s5_d[l], s5_w_glu[l],
                                s5_b_glu[l]), s5_norm[l])
        y_ret = retention_mixer(q, k, v, g, cos, sin, ret_norm[l])
        mixed = jnp.concatenate([y_lru, y_s5, y_ret], axis=-1)
        h = h + jnp.einsum('ble,ed->bld', mixed, w_out[l])
        xn = rmsnorm(h, norm_ffn[l])
        h = h + gated_conv_ffn(xn, w_up[l], ffn_conv_w[l], ffn_conv_b[l], w_down[l])
    return rmsnorm(h, norm_final)
```

```python
import functools
import math

import jax
import jax.numpy as jnp
from jax import lax
from jax.experimental import pallas as pl
from jax.experimental.pallas import tpu as pltpu

D_MODEL = 1024
BATCH = 8
SEQ = 2048
DEPTH = 2
ROWS = BATCH * SEQ

LRU_WIDTH = 512
LRU_BLOCKS = 8
LRU_CONV = 4
LRU_C = 8.0
S5_WIDTH = 512
S5_GROUP = 16
S5_GROUPS = 32
S5_STATE = 64
S5_STATES = S5_GROUPS * S5_STATE
RET_HEADS = 4
RET_HEAD_DIM = 128
RET_WIDTH = 512
ROPE_BASE = 10000.0
MIX_A_WIDTH = LRU_WIDTH + S5_WIDTH
PROJ_A_WIDTH = 2 * LRU_WIDTH + S5_WIDTH
QKVG_WIDTH = 4 * RET_WIDTH
IN_WIDTH = PROJ_A_WIDTH + QKVG_WIDTH
D_FF = 3 * D_MODEL
FFN_CONV = 3
NORM_EPS = 1e-6

SUBLANES = 8
LANES = 128
VMEM_LIMIT_BYTES = 56 * 1024 * 1024

MIX_TILE_STEPS = 64
MIX_TILE_ROWS = MIX_TILE_STEPS * BATCH
FFN_TILE_STEPS = 64
FFN_TILE_ROWS = FFN_TILE_STEPS * BATCH
FFN_CHUNK = 512
RET_CHUNK = 256
S5_SCAN_LANES = 512

_F32 = jnp.float32
_BF16 = jnp.bfloat16


def _gelu(x):
    return 0.5 * x * (1.0 + jnp.tanh(0.7978845608028654 * (x + 0.044715 * (x * x * x))))


def _sigmoid(x):
    return 1.0 / (1.0 + jnp.exp(-x))


def _rmsnorm(x, gain):
    return x * lax.rsqrt(jnp.mean(x * x, axis=-1, keepdims=True) + NORM_EPS) * gain


def _dot(a, b):
    return jnp.dot(a, b, preferred_element_type=_F32)


def _rotary_kernel(pos_ref, inv_ref, sign_ref, cos_ref, sin_ref):
    ang = pos_ref[0].astype(_F32) * inv_ref[...]
    cos_ref[0] = jnp.cos(ang)
    sin_ref[0] = jnp.sin(ang) * sign_ref[...]


def _rotary_tables(positions):
    half = RET_HEAD_DIM // 2
    inv = ROPE_BASE ** (-jnp.arange(half, dtype=_F32) * 2.0 / RET_HEAD_DIM)
    inv2 = jnp.concatenate([inv, inv])[None, :]
    sign = jnp.concatenate([-jnp.ones((half,), _F32), jnp.ones((half,), _F32)])[None, :]
    pos3 = positions.reshape(BATCH, SEQ, 1)
    out = jax.ShapeDtypeStruct((BATCH, SEQ, RET_HEAD_DIM), _F32)
    return pl.pallas_call(
        _rotary_kernel,
        grid=(BATCH,),
        in_specs=[pl.BlockSpec((1, SEQ, 1), lambda b: (b, 0, 0)),
                  pl.BlockSpec((1, RET_HEAD_DIM), lambda b: (0, 0)),
                  pl.BlockSpec((1, RET_HEAD_DIM), lambda b: (0, 0))],
        out_specs=[pl.BlockSpec((1, SEQ, RET_HEAD_DIM), lambda b: (b, 0, 0)),
                   pl.BlockSpec((1, SEQ, RET_HEAD_DIM), lambda b: (b, 0, 0))],
        out_shape=[out, out],
        name="rotary_tables",
    )(pos3, inv2, sign)


def _mix_kernel(h_ref, gain_ref, w_in_ref,
                convw_ref, convb_ref, wa_ref, ba_ref, wx_ref, bx_ref, sp_ref, lrun_ref,
                are_ref, aim_ref, wbre_ref, wbim_ref, cre_ref, cim_ref, d_ref,
                wglu_ref, bglu_ref, s5n_ref,
                ya_ref, qkvg_ref,
                xext, gate_s, u_s, la_s, lb_s, lru_state, sr, si, s5_state_r, s5_state_i):
    tm = MIX_TILE_ROWS
    halo = (LRU_CONV - 1) * BATCH

    @pl.when(pl.program_id(0) == 0)
    def _init():
        xext[0:halo, :] = jnp.zeros((halo, LRU_WIDTH), _F32)
        lru_state[...] = jnp.zeros_like(lru_state)
        s5_state_r[...] = jnp.zeros_like(s5_state_r)
        s5_state_i[...] = jnp.zeros_like(s5_state_i)

    xn = _rmsnorm(h_ref[...], gain_ref[...]).astype(_BF16)

    xext[halo:halo + tm, :] = _dot(xn, w_in_ref[:, 0:LRU_WIDTH])
    gate_s[...] = _dot(xn, w_in_ref[:, LRU_WIDTH:2 * LRU_WIDTH])
    u_s[...] = _dot(xn, w_in_ref[:, 2 * LRU_WIDTH:PROJ_A_WIDTH])
    for j in range(QKVG_WIDTH // 512):
        qkvg_ref[:, j * 512:(j + 1) * 512] = _dot(
            xn, w_in_ref[:, PROJ_A_WIDTH + j * 512:PROJ_A_WIDTH + (j + 1) * 512])

    xc = convb_ref[...]
    for k in range(LRU_CONV):
        xc = xc + convw_ref[k:k + 1, :] * xext[k * BATCH:k * BATCH + tm, :]
    xext[0:halo, :] = xext[tm:tm + halo, :]
    xcb = xc.astype(_BF16)
    r = _sigmoid(_dot(xcb, wa_ref[...]) + ba_ref[...])
    i = _sigmoid(_dot(xcb, wx_ref[...]) + bx_ref[...])
    log_a = (-LRU_C) * r * sp_ref[...]
    a = jnp.exp(log_a)
    la_s[...] = a
    lb_s[...] = jnp.sqrt(-jnp.tanh(log_a) * (1.0 + a * a)) * (i * xc)

    def lru_step(t, hs):
        row = pl.multiple_of(t * BATCH, BATCH)
        hs = la_s[pl.ds(row, BATCH), :] * hs + lb_s[pl.ds(row, BATCH), :]
        lb_s[pl.ds(row, BATCH), :] = hs
        return hs

    lru_state[...] = lax.fori_loop(0, MIX_TILE_STEPS, lru_step, lru_state[...], unroll=8)
    y_lru = lb_s[...] * _gelu(gate_s[...])
    ya_ref[:, 0:LRU_WIDTH] = _rmsnorm(y_lru, lrun_ref[...]).astype(ya_ref.dtype)

    ub = u_s[...].astype(_BF16)
    n_bt = S5_STATES // 256
    for j in range(n_bt):
        slab = ub[:, LANES * (j // 2):LANES * (j // 2 + 1)]
        sr[:, 256 * j:256 * (j + 1)] = _dot(slab, wbre_ref[j])
        si[:, 256 * j:256 * (j + 1)] = _dot(slab, wbim_ref[j])

    for sg in range(S5_STATES // S5_SCAN_LANES):
        sl = slice(sg * S5_SCAN_LANES, (sg + 1) * S5_SCAN_LANES)
        ar = jnp.broadcast_to(are_ref[:, sl], (BATCH, S5_SCAN_LANES))
        ai = jnp.broadcast_to(aim_ref[:, sl], (BATCH, S5_SCAN_LANES))

        def s5_step(t, carry, sl=sl, ar=ar, ai=ai):
            xr, xi = carry
            row = pl.multiple_of(t * BATCH, BATCH)
            nxr = ar * xr - ai * xi + sr[pl.ds(row, BATCH), sl]
            nxi = ar * xi + ai * xr + si[pl.ds(row, BATCH), sl]
            sr[pl.ds(row, BATCH), sl] = nxr
            si[pl.ds(row, BATCH), sl] = nxi
            return nxr, nxi

        xr, xi = lax.fori_loop(0, MIX_TILE_STEPS, s5_step,
                               (s5_state_r[:, sl], s5_state_i[:, sl]), unroll=8)
        s5_state_r[:, sl] = xr
        s5_state_i[:, sl] = xi

    n_slab = S5_WIDTH // LANES
    ys = []
    for s in range(n_slab):
        ksl = slice(512 * s, 512 * (s + 1))
        ys.append(_dot(sr[:, ksl].astype(_BF16), cre_ref[s])
                  - _dot(si[:, ksl].astype(_BF16), cim_ref[s]))
    y = jnp.concatenate(ys, axis=-1) + d_ref[...] * u_s[...]
    z = _gelu(y)
    out = z * _sigmoid(_dot(z.astype(_BF16), wglu_ref[...]) + bglu_ref[...])
    ya_ref[:, LRU_WIDTH:MIX_A_WIDTH] = _rmsnorm(out, s5n_ref[...]).astype(ya_ref.dtype)


def _const_spec(shape):
    nd = len(shape)
    return pl.BlockSpec(shape, lambda *_: (0,) * nd, pipeline_mode=pl.Buffered(1))


def _mix_call(h, gain, w_in, lru, s5):
    tm = MIX_TILE_ROWS
    consts = [gain, w_in, *lru, *s5]
    halo = (LRU_CONV - 1) * BATCH
    return pl.pallas_call(
        _mix_kernel,
        grid=(ROWS // tm,),
        in_specs=[pl.BlockSpec((tm, D_MODEL), lambda i: (i, 0))] + [_const_spec(c.shape) for c in consts],
        out_specs=[pl.BlockSpec((tm, MIX_A_WIDTH), lambda i: (i, 0)),
                   pl.BlockSpec((tm, QKVG_WIDTH), lambda i: (i, 0))],
        out_shape=[jax.ShapeDtypeStruct((ROWS, MIX_A_WIDTH), _BF16),
                   jax.ShapeDtypeStruct((ROWS, QKVG_WIDTH), _F32)],
        scratch_shapes=[pltpu.VMEM((tm + halo, LRU_WIDTH), _F32),
                        pltpu.VMEM((tm, LRU_WIDTH), _F32),
                        pltpu.VMEM((tm, S5_WIDTH), _F32),
                        pltpu.VMEM((tm, LRU_WIDTH), _F32),
                        pltpu.VMEM((tm, LRU_WIDTH), _F32),
                        pltpu.VMEM((BATCH, LRU_WIDTH), _F32),
                        pltpu.VMEM((tm, S5_STATES), _F32),
                        pltpu.VMEM((tm, S5_STATES), _F32),
                        pltpu.VMEM((BATCH, S5_STATES), _F32),
                        pltpu.VMEM((BATCH, S5_STATES), _F32)],
        compiler_params=pltpu.CompilerParams(dimension_semantics=("arbitrary",),
                                             vmem_limit_bytes=VMEM_LIMIT_BYTES),
        name="mix_lru_s5",
    )(h, *consts)


def _ret_kernel(q_ref, k_ref, v_ref, g_ref, cos_ref, sin_ref, gn_ref, o_ref, state_ref):
    C = RET_CHUNK
    Dh = RET_HEAD_DIM

    @pl.when(pl.program_id(1) == 0)
    def _init():
        state_ref[...] = jnp.zeros_like(state_ref)

    cos2 = cos_ref[0]
    sin2 = sin_ref[0]
    row = lax.broadcasted_iota(jnp.int32, (C, C), 0)
    col = lax.broadcasted_iota(jnp.int32, (C, C), 1)
    rel = (row - col).astype(_F32)
    idx = lax.broadcasted_iota(jnp.int32, (C, 1), 0).astype(_F32)

    def rot(t):
        return t * cos2 + pltpu.roll(t, Dh // 2, 1) * sin2

    for h in range(RET_HEADS):
        log_gamma = math.log1p(-(2.0 ** (-5.0 - h)))
        hs = slice(h * Dh, (h + 1) * Dh)
        qh = rot(q_ref[:, hs])
        kh = rot(k_ref[:, hs]) * (Dh ** -0.5)
        vb = v_ref[:, hs].astype(_BF16)
        decay = jnp.where(rel >= 0.0, jnp.exp(log_gamma * jnp.maximum(rel, 0.0)), 0.0)
        scores = lax.dot_general(qh.astype(_BF16), kh.astype(_BF16), (((1,), (1,)), ((), ())),
                                 preferred_element_type=_F32) * decay
        intra = _dot(scores.astype(_BF16), vb)
        state = state_ref[h]
        qd = qh * jnp.exp(log_gamma * (idx + 1.0))
        cross = _dot(qd.astype(_BF16), state.astype(_BF16))
        kd = kh * jnp.exp(log_gamma * (C - 1.0 - idx))
        kv = lax.dot_general(kd.astype(_BF16), vb, (((0,), (0,)), ((), ())),
                             preferred_element_type=_F32)
        state_ref[h] = math.exp(log_gamma * C) * state + kv
        o = intra + cross
        mu = jnp.mean(o, axis=-1, keepdims=True)
        oc = o - mu
        var = jnp.mean(oc * oc, axis=-1, keepdims=True)
        on = oc * lax.rsqrt(var + NORM_EPS) * gn_ref[:, hs]
        g = g_ref[:, hs]
        o_ref[:, hs] = (on * (g * _sigmoid(g))).astype(o_ref.dtype)


def _ret_call(qkvg, cos2, sin2, gn_gain):
    C = RET_CHUNK
    wide = qkvg.reshape(SEQ, BATCH * QKVG_WIDTH)
    blocks_per_batch = QKVG_WIDTH // RET_WIDTH

    def part(j):
        return pl.BlockSpec((C, RET_WIDTH), lambda b, n, j=j: (n, b * blocks_per_batch + j))

    out = pl.pallas_call(
        _ret_kernel,
        grid=(BATCH, SEQ // C),
        in_specs=[part(0), part(1), part(2), part(3),
                  pl.BlockSpec((1, C, RET_HEAD_DIM), lambda b, n: (b, n, 0)),
                  pl.BlockSpec((1, C, RET_HEAD_DIM), lambda b, n: (b, n, 0)),
                  pl.BlockSpec((1, RET_WIDTH), lambda b, n: (0, 0))],
        out_specs=pl.BlockSpec((C, RET_WIDTH), lambda b, n: (n, b)),
        out_shape=jax.ShapeDtypeStruct((SEQ, BATCH * RET_WIDTH), _BF16),
        scratch_shapes=[pltpu.VMEM((RET_HEADS, RET_HEAD_DIM, RET_HEAD_DIM), _F32)],
        compiler_params=pltpu.CompilerParams(dimension_semantics=("arbitrary", "arbitrary"),
                                             vmem_limit_bytes=VMEM_LIMIT_BYTES),
        name="retention",
    )(wide, wide, wide, wide, cos2, sin2, gn_gain)
    return out.reshape(ROWS, RET_WIDTH)


def _ffn_kernel(h_ref, ya_ref, yr_ref, wo_a_ref, wo_r_ref, gain_ref, w_up_ref, convw_ref, convb_ref,
                w_down_ref, fin_ref, o_ref, ext_v, ext_g, carry, *, final_norm):
    tm = FFN_TILE_ROWS
    halo = (FFN_CONV - 1) * BATCH
    fc = FFN_CHUNK

    @pl.when(pl.program_id(0) == 0)
    def _init():
        carry[...] = jnp.zeros_like(carry)

    h1 = _dot(ya_ref[...], wo_a_ref[...]) + _dot(yr_ref[...], wo_r_ref[...]) + h_ref[...]
    o_ref[...] = h1
    xn = _rmsnorm(h1, gain_ref[...]).astype(_BF16)
    for j in range(D_FF // fc):
        vs = slice(j * fc, (j + 1) * fc)
        gs = slice(D_FF + j * fc, D_FF + (j + 1) * fc)
        ext_v[0:halo, :] = carry[:, vs]
        ext_g[0:halo, :] = carry[:, gs]
        ext_v[halo:halo + tm, :] = _dot(xn, w_up_ref[:, vs])
        ext_g[halo:halo + tm, :] = _dot(xn, w_up_ref[:, gs])
        carry[:, vs] = ext_v[tm:tm + halo, :]
        carry[:, gs] = ext_g[tm:tm + halo, :]
        cv = convb_ref[:, vs]
        cg = convb_ref[:, gs]
        for k in range(FFN_CONV):
            cv = cv + convw_ref[k:k + 1, vs] * ext_v[k * BATCH:k * BATCH + tm, :]
            cg = cg + convw_ref[k:k + 1, gs] * ext_g[k * BATCH:k * BATCH + tm, :]
        act = (_gelu(cg) * cv).astype(_BF16)
        o_ref[...] += _dot(act, w_down_ref[vs, :])
    if final_norm:
        o_ref[...] = _rmsnorm(o_ref[...], fin_ref[...])


def _ffn_call(h, ya, yr, wo_a, wo_r, gain, w_up, convw, convb, w_down, fin_gain, final_norm):
    tm = FFN_TILE_ROWS
    halo = (FFN_CONV - 1) * BATCH
    consts = [wo_a, wo_r, gain, w_up, convw, convb, w_down, fin_gain]
    return pl.pallas_call(
        functools.partial(_ffn_kernel, final_norm=final_norm),
        grid=(ROWS // tm,),
        in_specs=[pl.BlockSpec((tm, D_MODEL), lambda i: (i, 0)),
                  pl.BlockSpec((tm, MIX_A_WIDTH), lambda i: (i, 0)),
                  pl.BlockSpec((tm, RET_WIDTH), lambda i: (i, 0))] + [_const_spec(c.shape) for c in consts],
        out_specs=pl.BlockSpec((tm, D_MODEL), lambda i: (i, 0)),
        out_shape=jax.ShapeDtypeStruct((ROWS, D_MODEL), _F32),
        scratch_shapes=[pltpu.VMEM((tm + halo, FFN_CHUNK), _F32),
                        pltpu.VMEM((tm + halo, FFN_CHUNK), _F32),
                        pltpu.VMEM((halo, 2 * D_FF), _F32)],
        compiler_params=pltpu.CompilerParams(dimension_semantics=("arbitrary",),
                                             vmem_limit_bytes=VMEM_LIMIT_BYTES),
        name="outproj_ffn",
    )(h, ya, yr, *consts)


def _block_diag(w):
    hh, ii, jj = w.shape
    eye = jnp.eye(hh, dtype=w.dtype)
    return jnp.einsum('hij,hg->higj', w, eye).reshape(hh * ii, hh * jj)


def _s5_params(lam_re, lam_im, log_dt, b_re, b_im, c_re, c_im):
    dt = jnp.exp(log_dt.astype(_F32))[:, None]
    lr, li = lam_re.astype(_F32), lam_im.astype(_F32)
    mag = jnp.exp(lr * dt)
    abar_re, abar_im = mag * jnp.cos(li * dt), mag * jnp.sin(li * dt)
    den = lr * lr + li * li
    nr, ni = abar_re - 1.0, abar_im
    coef_re = ((nr * lr + ni * li) / den)[..., None]
    coef_im = ((ni * lr - nr * li) / den)[..., None]
    br, bi = b_re.astype(_F32), b_im.astype(_F32)
    bbar_re = coef_re * br - coef_im * bi
    bbar_im = coef_re * bi + coef_im * br

    def pack_b(bbar):
        dense = _block_diag(bbar.transpose(0, 2, 1))
        tiles = [dense[LANES * (j // 2):LANES * (j // 2 + 1), 256 * j:256 * (j + 1)]
                 for j in range(S5_STATES // 256)]
        return jnp.stack(tiles).astype(_BF16)

    def pack_c(c):
        dense = _block_diag(c.astype(_F32).transpose(0, 2, 1))
        tiles = [dense[512 * s:512 * (s + 1), LANES * s:LANES * (s + 1)]
                 for s in range(S5_WIDTH // LANES)]
        return jnp.stack(tiles).astype(_BF16)

    return (abar_re.reshape(1, S5_STATES), abar_im.reshape(1, S5_STATES),
            pack_b(bbar_re), pack_b(bbar_im), pack_c(c_re), pack_c(c_im))


def kernel(x, positions, norm_mix, w_in, lru_conv_w, lru_conv_b, lru_wa, lru_ba, lru_wx, lru_bx, lru_lambda, lru_norm, s5_lambda_re, s5_lambda_im, s5_log_dt, s5_b_re, s5_b_im, s5_c_re, s5_c_im, s5_d, s5_w_glu, s5_b_glu, s5_norm, ret_norm, w_out, norm_ffn, w_up, ffn_conv_w, ffn_conv_b, w_down, norm_final):
    cos2, sin2 = _rotary_tables(positions)
    h = x.transpose(1, 0, 2).reshape(ROWS, D_MODEL)
    row = lambda v: v.reshape(1, -1).astype(_F32)
    for l in range(DEPTH):
        lru = (lru_conv_w[l], row(lru_conv_b[l]),
               _block_diag(lru_wa[l]).astype(_BF16), row(lru_ba[l]),
               _block_diag(lru_wx[l]).astype(_BF16), row(lru_bx[l]),
               row(jax.nn.softplus(-lru_lambda[l].astype(_F32))), row(lru_norm[l]))
        s5 = (*_s5_params(s5_lambda_re[l], s5_lambda_im[l], s5_log_dt[l], s5_b_re[l], s5_b_im[l],
                          s5_c_re[l], s5_c_im[l]),
              row(s5_d[l]), s5_w_glu[l].astype(_BF16), row(s5_b_glu[l]), row(s5_norm[l]))
        ya, qkvg = _mix_call(h, row(norm_mix[l]), w_in[l].astype(_BF16), lru, s5)
        yr = _ret_call(qkvg, cos2, sin2, row(ret_norm[l]))
        wo = w_out[l].astype(_BF16)
        h = _ffn_call(h, ya, yr, wo[:MIX_A_WIDTH], wo[MIX_A_WIDTH:], row(norm_ffn[l]),
                      w_up[l].astype(_BF16), ffn_conv_w[l], row(ffn_conv_b[l]),
                      w_down[l].astype(_BF16), row(norm_final), final_norm=(l == DEPTH - 1))
    return h.reshape(SEQ, BATCH, D_MODEL).transpose(1, 0, 2)
```

```python
import functools
import math

import jax
import jax.numpy as jnp
from jax import lax
from jax.experimental import pallas as pl
from jax.experimental.pallas import tpu as pltpu

D_MODEL = 1024
BATCH = 8
SEQ = 2048
DEPTH = 2
ROWS = BATCH * SEQ

LRU_WIDTH = 512
LRU_BLOCKS = 8
LRU_CONV = 4
LRU_C = 8.0
S5_WIDTH = 512
S5_GROUP = 16
S5_GROUPS = 32
S5_STATE = 64
S5_STATES = S5_GROUPS * S5_STATE
RET_HEADS = 4
RET_HEAD_DIM = 128
RET_WIDTH = 512
ROPE_BASE = 10000.0
MIX_A_WIDTH = LRU_WIDTH + S5_WIDTH
PROJ_A_WIDTH = 2 * LRU_WIDTH + S5_WIDTH
QKVG_WIDTH = 4 * RET_WIDTH
IN_WIDTH = PROJ_A_WIDTH + QKVG_WIDTH
D_FF = 3 * D_MODEL
FFN_CONV = 3
NORM_EPS = 1e-6

SUBLANES = 8
LANES = 128
VMEM_LIMIT_BYTES = 56 * 1024 * 1024

MIX_TILE_STEPS = 64
MIX_TILE_ROWS = MIX_TILE_STEPS * BATCH
FFN_TILE_STEPS = 64
FFN_TILE_ROWS = FFN_TILE_STEPS * BATCH
FFN_CHUNK = 512
RET_CHUNK = 256
S5_SCAN_LANES = 512

_F32 = jnp.float32
_BF16 = jnp.bfloat16


def _gelu(x):
    return 0.5 * x * (1.0 + jnp.tanh(0.7978845608028654 * (x + 0.044715 * (x * x * x))))


def _sigmoid(x):
    return 1.0 / (1.0 + jnp.exp(-x))


def _rmsnorm(x, gain):
    return x * lax.rsqrt(jnp.mean(x * x, axis=-1, keepdims=True) + NORM_EPS) * gain


def _dot(a, b):
    return jnp.dot(a, b, preferred_element_type=_F32)


def _rotary_kernel(pos_ref, inv_ref, sign_ref, cos_ref, sin_ref):
    ang = pos_ref[0].astype(_F32) * inv_ref[...]
    cos_ref[0] = jnp.cos(ang)
    sin_ref[0] = jnp.sin(ang) * sign_ref[...]


def _rotary_tables(positions):
    half = RET_HEAD_DIM // 2
    inv = ROPE_BASE ** (-jnp.arange(half, dtype=_F32) * 2.0 / RET_HEAD_DIM)
    inv2 = jnp.concatenate([inv, inv])[None, :]
    sign = jnp.concatenate([-jnp.ones((half,), _F32), jnp.ones((half,), _F32)])[None, :]
    pos3 = positions.reshape(BATCH, SEQ, 1)
    out = jax.ShapeDtypeStruct((BATCH, SEQ, RET_HEAD_DIM), _F32)
    return pl.pallas_call(
        _rotary_kernel,
        grid=(BATCH,),
        in_specs=[pl.BlockSpec((1, SEQ, 1), lambda b: (b, 0, 0)),
                  pl.BlockSpec((1, RET_HEAD_DIM), lambda b: (0, 0)),
                  pl.BlockSpec((1, RET_HEAD_DIM), lambda b: (0, 0))],
        out_specs=[pl.BlockSpec((1, SEQ, RET_HEAD_DIM), lambda b: (b, 0, 0)),
                   pl.BlockSpec((1, SEQ, RET_HEAD_DIM), lambda b: (b, 0, 0))],
        out_shape=[out, out],
        name="rotary_tables",
    )(pos3, inv2, sign)


def _mix_kernel(h_ref, gain_ref, w_in_ref,
                convw_ref, convb_ref, wa_ref, ba_ref, wx_ref, bx_ref, sp_ref, lrun_ref,
                are_ref, aim_ref, wbre_ref, wbim_ref, cre_ref, cim_ref, d_ref,
                wglu_ref, bglu_ref, s5n_ref,
                ya_ref, qkvg_ref,
                xext, gate_s, u_s, la_s, lb_s, lru_state, sr, si, s5_state_r, s5_state_i, qslab):
    tm = MIX_TILE_ROWS
    halo = (LRU_CONV - 1) * BATCH

    @pl.when(pl.program_id(0) == 0)
    def _init():
        xext[0:halo, :] = jnp.zeros((halo, LRU_WIDTH), _F32)
        lru_state[...] = jnp.zeros_like(lru_state)
        s5_state_r[...] = jnp.zeros_like(s5_state_r)
        s5_state_i[...] = jnp.zeros_like(s5_state_i)

    xn = _rmsnorm(h_ref[...], gain_ref[...]).astype(_BF16)

    xext[halo:halo + tm, :] = _dot(xn, w_in_ref[:, 0:LRU_WIDTH])
    gate_s[...] = _dot(xn, w_in_ref[:, LRU_WIDTH:2 * LRU_WIDTH])
    u_s[...] = _dot(xn, w_in_ref[:, 2 * LRU_WIDTH:PROJ_A_WIDTH])
    for j in range(QKVG_WIDTH // 512):
        res = _dot(xn, w_in_ref[:, PROJ_A_WIDTH + j * 512:PROJ_A_WIDTH + (j + 1) * 512])
        for s in range(512 // LANES):
            qslab[j * (512 // LANES) + s] = res[:, s * LANES:(s + 1) * LANES]
    for b in range(BATCH):
        for s in range(QKVG_WIDTH // LANES):
            qkvg_ref[b, :, s * LANES:(s + 1) * LANES] = qslab[s, pl.ds(b, MIX_TILE_STEPS, stride=BATCH), :]

    xc = convb_ref[...]
    for k in range(LRU_CONV):
        xc = xc + convw_ref[k:k + 1, :] * xext[k * BATCH:k * BATCH + tm, :]
    xext[0:halo, :] = xext[tm:tm + halo, :]
    xcb = xc.astype(_BF16)
    r = _sigmoid(_dot(xcb, wa_ref[...]) + ba_ref[...])
    i = _sigmoid(_dot(xcb, wx_ref[...]) + bx_ref[...])
    log_a = (-LRU_C) * r * sp_ref[...]
    a = jnp.exp(log_a)
    la_s[...] = a
    lb_s[...] = jnp.sqrt(-jnp.tanh(log_a) * (1.0 + a * a)) * (i * xc)

    def lru_step(t, hs):
        row = pl.multiple_of(t * BATCH, BATCH)
        hs = la_s[pl.ds(row, BATCH), :] * hs + lb_s[pl.ds(row, BATCH), :]
        lb_s[pl.ds(row, BATCH), :] = hs
        return hs

    lru_state[...] = lax.fori_loop(0, MIX_TILE_STEPS, lru_step, lru_state[...], unroll=8)
    y_lru = lb_s[...] * _gelu(gate_s[...])
    ya_ref[:, 0:LRU_WIDTH] = _rmsnorm(y_lru, lrun_ref[...]).astype(ya_ref.dtype)

    ub = u_s[...].astype(_BF16)
    n_bt = S5_STATES // 256
    for j in range(n_bt):
        slab = ub[:, LANES * (j // 2):LANES * (j // 2 + 1)]
        sr[:, 256 * j:256 * (j + 1)] = _dot(slab, wbre_ref[j])
        si[:, 256 * j:256 * (j + 1)] = _dot(slab, wbim_ref[j])

    for sg in range(S5_STATES // S5_SCAN_LANES):
        sl = slice(sg * S5_SCAN_LANES, (sg + 1) * S5_SCAN_LANES)
        ar = jnp.broadcast_to(are_ref[:, sl], (BATCH, S5_SCAN_LANES))
        ai = jnp.broadcast_to(aim_ref[:, sl], (BATCH, S5_SCAN_LANES))

        def s5_step(t, carry, sl=sl, ar=ar, ai=ai):
            xr, xi = carry
            row = pl.multiple_of(t * BATCH, BATCH)
            nxr = ar * xr - ai * xi + sr[pl.ds(row, BATCH), sl]
            nxi = ar * xi + ai * xr + si[pl.ds(row, BATCH), sl]
            sr[pl.ds(row, BATCH), sl] = nxr
            si[pl.ds(row, BATCH), sl] = nxi
            return nxr, nxi

        xr, xi = lax.fori_loop(0, MIX_TILE_STEPS, s5_step,
                               (s5_state_r[:, sl], s5_state_i[:, sl]), unroll=8)
        s5_state_r[:, sl] = xr
        s5_state_i[:, sl] = xi

    n_slab = S5_WIDTH // LANES
    ys = []
    for s in range(n_slab):
        ksl = slice(512 * s, 512 * (s + 1))
        ys.append(_dot(sr[:, ksl].astype(_BF16), cre_ref[s])
                  - _dot(si[:, ksl].astype(_BF16), cim_ref[s]))
    y = jnp.concatenate(ys, axis=-1) + d_ref[...] * u_s[...]
    z = _gelu(y)
    out = z * _sigmoid(_dot(z.astype(_BF16), wglu_ref[...]) + bglu_ref[...])
    ya_ref[:, LRU_WIDTH:MIX_A_WIDTH] = _rmsnorm(out, s5n_ref[...]).astype(ya_ref.dtype)


def _const_spec(shape):
    nd = len(shape)
    return pl.BlockSpec(shape, lambda *_: (0,) * nd, pipeline_mode=pl.Buffered(1))


def _mix_call(h, gain, w_in, lru, s5):
    tm = MIX_TILE_ROWS
    consts = [gain, w_in, *lru, *s5]
    halo = (LRU_CONV - 1) * BATCH
    return pl.pallas_call(
        _mix_kernel,
        grid=(ROWS // tm,),
        in_specs=[pl.BlockSpec((tm, D_MODEL), lambda i: (i, 0))] + [_const_spec(c.shape) for c in consts],
        out_specs=[pl.BlockSpec((tm, MIX_A_WIDTH), lambda i: (i, 0)),
                   pl.BlockSpec((BATCH, MIX_TILE_STEPS, QKVG_WIDTH), lambda i: (0, i, 0))],
        out_shape=[jax.ShapeDtypeStruct((ROWS, MIX_A_WIDTH), _BF16),
                   jax.ShapeDtypeStruct((BATCH, SEQ, QKVG_WIDTH), _F32)],
        scratch_shapes=[pltpu.VMEM((tm + halo, LRU_WIDTH), _F32),
                        pltpu.VMEM((tm, LRU_WIDTH), _F32),
                        pltpu.VMEM((tm, S5_WIDTH), _F32),
                        pltpu.VMEM((tm, LRU_WIDTH), _F32),
                        pltpu.VMEM((tm, LRU_WIDTH), _F32),
                        pltpu.VMEM((BATCH, LRU_WIDTH), _F32),
                        pltpu.VMEM((tm, S5_STATES), _F32),
                        pltpu.VMEM((tm, S5_STATES), _F32),
                        pltpu.VMEM((BATCH, S5_STATES), _F32),
                        pltpu.VMEM((BATCH, S5_STATES), _F32),
                        pltpu.VMEM((QKVG_WIDTH // LANES, tm, LANES), _F32)],
        compiler_params=pltpu.CompilerParams(dimension_semantics=("arbitrary",),
                                             vmem_limit_bytes=VMEM_LIMIT_BYTES),
        name="mix_lru_s5",
    )(h, *consts)


def _ret_kernel(q_ref, k_ref, v_ref, g_ref, cos_ref, sin_ref, gn_ref, o_ref, state_ref):
    C = RET_CHUNK
    Dh = RET_HEAD_DIM

    @pl.when(pl.program_id(1) == 0)
    def _init():
        state_ref[...] = jnp.zeros_like(state_ref)

    cos2 = cos_ref[0]
    sin2 = sin_ref[0]
    row = lax.broadcasted_iota(jnp.int32, (C, C), 0)
    col = lax.broadcasted_iota(jnp.int32, (C, C), 1)
    rel = (row - col).astype(_F32)
    idx = lax.broadcasted_iota(jnp.int32, (C, 1), 0).astype(_F32)

    def rot(t):
        return t * cos2 + pltpu.roll(t, Dh // 2, 1) * sin2

    for h in range(RET_HEADS):
        log_gamma = math.log1p(-(2.0 ** (-5.0 - h)))
        hs = slice(h * Dh, (h + 1) * Dh)
        qh = rot(q_ref[:, hs])
        kh = rot(k_ref[:, hs]) * (Dh ** -0.5)
        vb = v_ref[:, hs].astype(_BF16)
        decay = jnp.where(rel >= 0.0, jnp.exp(log_gamma * jnp.maximum(rel, 0.0)), 0.0)
        scores = lax.dot_general(qh.astype(_BF16), kh.astype(_BF16), (((1,), (1,)), ((), ())),
                                 preferred_element_type=_F32) * decay
        intra = _dot(scores.astype(_BF16), vb)
        state = state_ref[h]
        qd = qh * jnp.exp(log_gamma * (idx + 1.0))
        cross = _dot(qd.astype(_BF16), state.astype(_BF16))
        kd = kh * jnp.exp(log_gamma * (C - 1.0 - idx))
        kv = lax.dot_general(kd.astype(_BF16), vb, (((0,), (0,)), ((), ())),
                             preferred_element_type=_F32)
        state_ref[h] = math.exp(log_gamma * C) * state + kv
        o = intra + cross
        mu = jnp.mean(o, axis=-1, keepdims=True)
        oc = o - mu
        var = jnp.mean(oc * oc, axis=-1, keepdims=True)
        on = oc * lax.rsqrt(var + NORM_EPS) * gn_ref[:, hs]
        g = g_ref[:, hs]
        o_ref[:, hs] = (on * (g * _sigmoid(g))).astype(o_ref.dtype)


def _ret_call(qkvg, cos2, sin2, gn_gain):
    C = RET_CHUNK

    def part(j):
        return pl.BlockSpec((None, C, RET_WIDTH), lambda b, n, j=j: (b, n, j))

    return pl.pallas_call(
        _ret_kernel,
        grid=(BATCH, SEQ // C),
        in_specs=[part(0), part(1), part(2), part(3),
                  pl.BlockSpec((1, C, RET_HEAD_DIM), lambda b, n: (b, n, 0)),
                  pl.BlockSpec((1, C, RET_HEAD_DIM), lambda b, n: (b, n, 0)),
                  pl.BlockSpec((1, RET_WIDTH), lambda b, n: (0, 0))],
        out_specs=pl.BlockSpec((None, C, RET_WIDTH), lambda b, n: (b, n, 0)),
        out_shape=jax.ShapeDtypeStruct((BATCH, SEQ, RET_WIDTH), _F32),
        scratch_shapes=[pltpu.VMEM((RET_HEADS, RET_HEAD_DIM, RET_HEAD_DIM), _F32)],
        compiler_params=pltpu.CompilerParams(dimension_semantics=("arbitrary", "arbitrary"),
                                             vmem_limit_bytes=VMEM_LIMIT_BYTES),
        name="retention",
    )(qkvg, qkvg, qkvg, qkvg, cos2, sin2, gn_gain)


def _ffn_kernel(h_ref, ya_ref, yr_ref, wo_a_ref, wo_r_ref, gain_ref, w_up_ref, convw_ref, convb_ref,
                w_down_ref, fin_ref, o_ref, ext_v, ext_g, carry, yslab, *, final_norm):
    tm = FFN_TILE_ROWS
    halo = (FFN_CONV - 1) * BATCH
    fc = FFN_CHUNK

    @pl.when(pl.program_id(0) == 0)
    def _init():
        carry[...] = jnp.zeros_like(carry)

    for b in range(BATCH):
        for s in range(RET_WIDTH // LANES):
            yslab[s, pl.ds(b, FFN_TILE_STEPS, stride=BATCH), :] = yr_ref[b, :, s * LANES:(s + 1) * LANES]
    yr = jnp.concatenate([yslab[s] for s in range(RET_WIDTH // LANES)], axis=-1).astype(_BF16)
    h1 = _dot(ya_ref[...], wo_a_ref[...]) + _dot(yr, wo_r_ref[...]) + h_ref[...]
    o_ref[...] = h1
    xn = _rmsnorm(h1, gain_ref[...]).astype(_BF16)
    for j in range(D_FF // fc):
        vs = slice(j * fc, (j + 1) * fc)
        gs = slice(D_FF + j * fc, D_FF + (j + 1) * fc)
        ext_v[0:halo, :] = carry[:, vs]
        ext_g[0:halo, :] = carry[:, gs]
        ext_v[halo:halo + tm, :] = _dot(xn, w_up_ref[:, vs])
        ext_g[halo:halo + tm, :] = _dot(xn, w_up_ref[:, gs])
        carry[:, vs] = ext_v[tm:tm + halo, :]
        carry[:, gs] = ext_g[tm:tm + halo, :]
        cv = convb_ref[:, vs]
        cg = convb_ref[:, gs]
        for k in range(FFN_CONV):
            cv = cv + convw_ref[k:k + 1, vs] * ext_v[k * BATCH:k * BATCH + tm, :]
            cg = cg + convw_ref[k:k + 1, gs] * ext_g[k * BATCH:k * BATCH + tm, :]
        act = (_gelu(cg) * cv).astype(_BF16)
        o_ref[...] += _dot(act, w_down_ref[vs, :])
    if final_norm:
        o_ref[...] = _rmsnorm(o_ref[...], fin_ref[...])


def _ffn_call(h, ya, yr, wo_a, wo_r, gain, w_up, convw, convb, w_down, fin_gain, final_norm):
    tm = FFN_TILE_ROWS
    halo = (FFN_CONV - 1) * BATCH
    consts = [wo_a, wo_r, gain, w_up, convw, convb, w_down, fin_gain]
    return pl.pallas_call(
        functools.partial(_ffn_kernel, final_norm=final_norm),
        grid=(ROWS // tm,),
        in_specs=[pl.BlockSpec((tm, D_MODEL), lambda i: (i, 0)),
                  pl.BlockSpec((tm, MIX_A_WIDTH), lambda i: (i, 0)),
                  pl.BlockSpec((BATCH, FFN_TILE_STEPS, RET_WIDTH), lambda i: (0, i, 0))]
                 + [_const_spec(c.shape) for c in consts],
        out_specs=pl.BlockSpec((tm, D_MODEL), lambda i: (i, 0)),
        out_shape=jax.ShapeDtypeStruct((ROWS, D_MODEL), _F32),
        scratch_shapes=[pltpu.VMEM((tm + halo, FFN_CHUNK), _F32),
                        pltpu.VMEM((tm + halo, FFN_CHUNK), _F32),
                        pltpu.VMEM((halo, 2 * D_FF), _F32),
                        pltpu.VMEM((RET_WIDTH // LANES, tm, LANES), _F32)],
        compiler_params=pltpu.CompilerParams(dimension_semantics=("arbitrary",),
                                             vmem_limit_bytes=VMEM_LIMIT_BYTES),
        name="outproj_ffn",
    )(h, ya, yr, *consts)


def _block_diag(w):
    hh, ii, jj = w.shape
    eye = jnp.eye(hh, dtype=w.dtype)
    return jnp.einsum('hij,hg->higj', w, eye).reshape(hh * ii, hh * jj)


def _s5_params(lam_re, lam_im, log_dt, b_re, b_im, c_re, c_im):
    dt = jnp.exp(log_dt.astype(_F32))[:, None]
    lr, li = lam_re.astype(_F32), lam_im.astype(_F32)
    mag = jnp.exp(lr * dt)
    abar_re, abar_im = mag * jnp.cos(li * dt), mag * jnp.sin(li * dt)
    den = lr * lr + li * li
    nr, ni = abar_re - 1.0, abar_im
    coef_re = ((nr * lr + ni * li) / den)[..., None]
    coef_im = ((ni * lr - nr * li) / den)[..., None]
    br, bi = b_re.astype(_F32), b_im.astype(_F32)
    bbar_re = coef_re * br - coef_im * bi
    bbar_im = coef_re * bi + coef_im * br

    def pack_b(bbar):
        dense = _block_diag(bbar.transpose(0, 2, 1))
        tiles = [dense[LANES * (j // 2):LANES * (j // 2 + 1), 256 * j:256 * (j + 1)]
                 for j in range(S5_STATES // 256)]
        return jnp.stack(tiles).astype(_BF16)

    def pack_c(c):
        dense = _block_diag(c.astype(_F32).transpose(0, 2, 1))
        tiles = [dense[512 * s:512 * (s + 1), LANES * s:LANES * (s + 1)]
                 for s in range(S5_WIDTH // LANES)]
        return jnp.stack(tiles).astype(_BF16)

    return (abar_re.reshape(1, S5_STATES), abar_im.reshape(1, S5_STATES),
            pack_b(bbar_re), pack_b(bbar_im), pack_c(c_re), pack_c(c_im))


def kernel(x, positions, norm_mix, w_in, lru_conv_w, lru_conv_b, lru_wa, lru_ba, lru_wx, lru_bx, lru_lambda, lru_norm, s5_lambda_re, s5_lambda_im, s5_log_dt, s5_b_re, s5_b_im, s5_c_re, s5_c_im, s5_d, s5_w_glu, s5_b_glu, s5_norm, ret_norm, w_out, norm_ffn, w_up, ffn_conv_w, ffn_conv_b, w_down, norm_final):
    cos2, sin2 = _rotary_tables(positions)
    h = x.transpose(1, 0, 2).reshape(ROWS, D_MODEL)
    row = lambda v: v.reshape(1, -1).astype(_F32)
    for l in range(DEPTH):
        lru = (lru_conv_w[l], row(lru_conv_b[l]),
               _block_diag(lru_wa[l]).astype(_BF16), row(lru_ba[l]),
               _block_diag(lru_wx[l]).astype(_BF16), row(lru_bx[l]),
               row(jax.nn.softplus(-lru_lambda[l].astype(_F32))), row(lru_norm[l]))
        s5 = (*_s5_params(s5_lambda_re[l], s5_lambda_im[l], s5_log_dt[l], s5_b_re[l], s5_b_im[l],
                          s5_c_re[l], s5_c_im[l]),
              row(s5_d[l]), s5_w_glu[l].astype(_BF16), row(s5_b_glu[l]), row(s5_norm[l]))
        ya, qkvg = _mix_call(h, row(norm_mix[l]), w_in[l].astype(_BF16), lru, s5)
        yr = _ret_call(qkvg, cos2, sin2, row(ret_norm[l]))
        wo = w_out[l].astype(_BF16)
        h = _ffn_call(h, ya, yr, wo[:MIX_A_WIDTH], wo[MIX_A_WIDTH:], row(norm_ffn[l]),
                      w_up[l].astype(_BF16), ffn_conv_w[l], row(ffn_conv_b[l]),
                      w_down[l].astype(_BF16), row(norm_final), final_norm=(l == DEPTH - 1))
    return h.reshape(SEQ, BATCH, D_MODEL).transpose(1, 0, 2)
```

```python
import functools
import math

import jax
import jax.numpy as jnp
from jax import lax
from jax.experimental import pallas as pl
from jax.experimental.pallas import tpu as pltpu

D_MODEL = 1024
BATCH = 8
SEQ = 2048
DEPTH = 2
ROWS = BATCH * SEQ

LRU_WIDTH = 512
LRU_BLOCKS = 8
LRU_CONV = 4
LRU_C = 8.0
S5_WIDTH = 512
S5_GROUP = 16
S5_GROUPS = 32
S5_STATE = 64
S5_STATES = S5_GROUPS * S5_STATE
RET_HEADS = 4
RET_HEAD_DIM = 128
RET_WIDTH = 512
ROPE_BASE = 10000.0
MIX_A_WIDTH = LRU_WIDTH + S5_WIDTH
PROJ_A_WIDTH = 2 * LRU_WIDTH + S5_WIDTH
QKVG_WIDTH = 4 * RET_WIDTH
IN_WIDTH = PROJ_A_WIDTH + QKVG_WIDTH
D_FF = 3 * D_MODEL
FFN_CONV = 3
NORM_EPS = 1e-6

SUBLANES = 8
LANES = 128
MXU_WIDTH = 256
VMEM_LIMIT_BYTES = 56 * 1024 * 1024

MIX_TILE_STEPS = 64
MIX_TILE_ROWS = MIX_TILE_STEPS * BATCH
FFN_TILE_STEPS = 64
FFN_TILE_ROWS = FFN_TILE_STEPS * BATCH
FFN_CHUNK = 1024
RET_CHUNK = 256
S5_SCAN_LANES = 512

_F32 = jnp.float32
_BF16 = jnp.bfloat16


def _gelu(x):
    return 0.5 * x * (1.0 + jnp.tanh(0.7978845608028654 * (x + 0.044715 * (x * x * x))))


def _sigmoid(x):
    return 1.0 / (1.0 + jnp.exp(-x))


def _rmsnorm(x, gain):
    return x * lax.rsqrt(jnp.mean(x * x, axis=-1, keepdims=True) + NORM_EPS) * gain


def _dot(a, b):
    return jnp.dot(a, b, preferred_element_type=_F32)


def _to_time_major(x_ref, slab, steps):
    n = x_ref.shape[-1] // LANES
    for b in range(BATCH):
        for s in range(n):
            slab[s, pl.ds(b, steps, stride=BATCH), :] = x_ref[b, :, s * LANES:(s + 1) * LANES]
    return jnp.concatenate([slab[s] for s in range(n)], axis=-1)


def _from_time_major(slab, o_ref, steps):
    n = o_ref.shape[-1] // LANES
    for b in range(BATCH):
        for s in range(n):
            o_ref[b, :, s * LANES:(s + 1) * LANES] = slab[s, pl.ds(b, steps, stride=BATCH), :]


def _causal_conv(x, hist, w_ref, bias, cols):
    taps = w_ref.shape[0]
    tm = x.shape[0]
    y = bias
    for k in range(taps):
        back = (taps - 1 - k) * BATCH
        xs = x if back == 0 else jnp.concatenate([hist[hist.shape[0] - back:], x[:tm - back]], axis=0)
        y = y + w_ref[k:k + 1, cols] * xs
    return y


def _rotary_kernel(pos_ref, inv_ref, sign_ref, cos_ref, sin_ref):
    ang = pos_ref[0].astype(_F32) * inv_ref[...]
    cos_ref[0] = jnp.cos(ang)
    sin_ref[0] = jnp.sin(ang) * sign_ref[...]


def _rotary_tables(positions):
    half = RET_HEAD_DIM // 2
    inv = ROPE_BASE ** (-jnp.arange(half, dtype=_F32) * 2.0 / RET_HEAD_DIM)
    inv2 = jnp.concatenate([inv, inv])[None, :]
    sign = jnp.concatenate([-jnp.ones((half,), _F32), jnp.ones((half,), _F32)])[None, :]
    pos3 = positions.reshape(BATCH, SEQ, 1)
    out = jax.ShapeDtypeStruct((BATCH, SEQ, RET_HEAD_DIM), _F32)
    return pl.pallas_call(
        _rotary_kernel,
        grid=(BATCH,),
        in_specs=[pl.BlockSpec((1, SEQ, 1), lambda b: (b, 0, 0)),
                  pl.BlockSpec((1, RET_HEAD_DIM), lambda b: (0, 0)),
                  pl.BlockSpec((1, RET_HEAD_DIM), lambda b: (0, 0))],
        out_specs=[pl.BlockSpec((1, SEQ, RET_HEAD_DIM), lambda b: (b, 0, 0)),
                   pl.BlockSpec((1, SEQ, RET_HEAD_DIM), lambda b: (b, 0, 0))],
        out_shape=[out, out],
        name="rotary_tables",
    )(pos3, inv2, sign)


def _mix_kernel(h_ref, gain_ref, w_in_ref,
                convw_ref, convb_ref, wa_ref, ba_ref, wx_ref, bx_ref, sp_ref, lrun_ref,
                are_ref, aim_ref, wbre_ref, wbim_ref, cre_ref, cim_ref, d_ref,
                wglu_ref, bglu_ref, s5n_ref,
                ya_ref, qkvg_ref,
                conv_hist, gate_s, u_s, la_s, lb_s, lru_state, sr, si, s5_state_r, s5_state_i, qslab, hslab,
                *, batch_major_in):
    tm = MIX_TILE_ROWS

    @pl.when(pl.program_id(0) == 0)
    def _init():
        conv_hist[...] = jnp.zeros_like(conv_hist)
        lru_state[...] = jnp.zeros_like(lru_state)
        s5_state_r[...] = jnp.zeros_like(s5_state_r)
        s5_state_i[...] = jnp.zeros_like(s5_state_i)

    h = _to_time_major(h_ref, hslab, MIX_TILE_STEPS) if batch_major_in else h_ref[...]
    xn = _rmsnorm(h, gain_ref[...]).astype(_BF16)

    u_s[...] = _dot(xn, w_in_ref[:, 2 * LRU_WIDTH:PROJ_A_WIDTH])
    ub = u_s[...].astype(_BF16)
    for j in range(S5_STATES // MXU_WIDTH):
        slab = ub[:, LANES * (j // 2):LANES * (j // 2 + 1)]
        sr[:, MXU_WIDTH * j:MXU_WIDTH * (j + 1)] = _dot(slab, wbre_ref[j])
        si[:, MXU_WIDTH * j:MXU_WIDTH * (j + 1)] = _dot(slab, wbim_ref[j])

    for sg in range(S5_STATES // S5_SCAN_LANES):
        sl = slice(sg * S5_SCAN_LANES, (sg + 1) * S5_SCAN_LANES)
        ar = jnp.broadcast_to(are_ref[:, sl], (BATCH, S5_SCAN_LANES))
        ai = jnp.broadcast_to(aim_ref[:, sl], (BATCH, S5_SCAN_LANES))
        xr = s5_state_r[:, sl]
        xi = s5_state_i[:, sl]
        for t in range(MIX_TILE_STEPS):
            rows = slice(t * BATCH, (t + 1) * BATCH)
            xr, xi = (ar * xr - ai * xi + sr[rows, sl], ar * xi + ai * xr + si[rows, sl])
            sr[rows, sl] = xr
            si[rows, sl] = xi
        s5_state_r[:, sl] = xr
        s5_state_i[:, sl] = xi

    lru_x = _dot(xn, w_in_ref[:, 0:LRU_WIDTH])
    gate_s[...] = _dot(xn, w_in_ref[:, LRU_WIDTH:2 * LRU_WIDTH])
    xc = _causal_conv(lru_x, conv_hist[...], convw_ref, convb_ref[...], slice(None))
    conv_hist[...] = lru_x[tm - (LRU_CONV - 1) * BATCH:]
    xcb = xc.astype(_BF16)
    halves = range(LRU_WIDTH // MXU_WIDTH)
    pre_r = jnp.concatenate([_dot(xcb[:, MXU_WIDTH * p:MXU_WIDTH * (p + 1)], wa_ref[p]) for p in halves], axis=-1)
    pre_i = jnp.concatenate([_dot(xcb[:, MXU_WIDTH * p:MXU_WIDTH * (p + 1)], wx_ref[p]) for p in halves], axis=-1)
    r = _sigmoid(pre_r + ba_ref[...])
    i = _sigmoid(pre_i + bx_ref[...])
    log_a = (-LRU_C) * r * sp_ref[...]
    a = jnp.exp(log_a)
    la_s[...] = a
    lb_s[...] = jnp.sqrt(-jnp.tanh(log_a) * (1.0 + a * a)) * (i * xc)

    ys = []
    for s in range(S5_WIDTH // LANES):
        ksl = slice(512 * s, 512 * (s + 1))
        ys.append(_dot(sr[:, ksl].astype(_BF16), cre_ref[s])
                  - _dot(si[:, ksl].astype(_BF16), cim_ref[s]))
    y = jnp.concatenate(ys, axis=-1) + d_ref[...] * u_s[...]
    z = _gelu(y)
    glu = _dot(z.astype(_BF16), wglu_ref[...])

    for j in range(QKVG_WIDTH // 512):
        res = _dot(xn, w_in_ref[:, PROJ_A_WIDTH + j * 512:PROJ_A_WIDTH + (j + 1) * 512])
        for s in range(512 // LANES):
            qslab[j * (512 // LANES) + s] = res[:, s * LANES:(s + 1) * LANES]

    hs = lru_state[...]
    for t in range(MIX_TILE_STEPS):
        rows = slice(t * BATCH, (t + 1) * BATCH)
        hs = la_s[rows, :] * hs + lb_s[rows, :]
        lb_s[rows, :] = hs
    lru_state[...] = hs
    y_lru = lb_s[...] * _gelu(gate_s[...])
    ya_ref[:, 0:LRU_WIDTH] = _rmsnorm(y_lru, lrun_ref[...]).astype(ya_ref.dtype)

    out = z * _sigmoid(glu + bglu_ref[...])
    ya_ref[:, LRU_WIDTH:MIX_A_WIDTH] = _rmsnorm(out, s5n_ref[...]).astype(ya_ref.dtype)
    _from_time_major(qslab, qkvg_ref, MIX_TILE_STEPS)


def _const_spec(shape):
    nd = len(shape)
    return pl.BlockSpec(shape, lambda *_: (0,) * nd, pipeline_mode=pl.Buffered(1))


def _stream_spec(batch_major, steps, width):
    if batch_major:
        return pl.BlockSpec((BATCH, steps, width), lambda i: (0, i, 0))
    return pl.BlockSpec((steps * BATCH, width), lambda i: (i, 0))


def _mix_call(h, gain, w_in, lru, s5, batch_major_in):
    tm = MIX_TILE_ROWS
    consts = [gain, w_in, *lru, *s5]
    return pl.pallas_call(
        functools.partial(_mix_kernel, batch_major_in=batch_major_in),
        grid=(ROWS // tm,),
        in_specs=[_stream_spec(batch_major_in, MIX_TILE_STEPS, D_MODEL)] + [_const_spec(c.shape) for c in consts],
        out_specs=[pl.BlockSpec((tm, MIX_A_WIDTH), lambda i: (i, 0)),
                   _stream_spec(True, MIX_TILE_STEPS, QKVG_WIDTH)],
        out_shape=[jax.ShapeDtypeStruct((ROWS, MIX_A_WIDTH), _BF16),
                   jax.ShapeDtypeStruct((BATCH, SEQ, QKVG_WIDTH), _F32)],
        scratch_shapes=[pltpu.VMEM(((LRU_CONV - 1) * BATCH, LRU_WIDTH), _F32),
                        pltpu.VMEM((tm, LRU_WIDTH), _F32),
                        pltpu.VMEM((tm, S5_WIDTH), _F32),
                        pltpu.VMEM((tm, LRU_WIDTH), _F32),
                        pltpu.VMEM((tm, LRU_WIDTH), _F32),
                        pltpu.VMEM((BATCH, LRU_WIDTH), _F32),
                        pltpu.VMEM((tm, S5_STATES), _F32),
                        pltpu.VMEM((tm, S5_STATES), _F32),
                        pltpu.VMEM((BATCH, S5_STATES), _F32),
                        pltpu.VMEM((BATCH, S5_STATES), _F32),
                        pltpu.VMEM((QKVG_WIDTH // LANES, tm, LANES), _F32),
                        pltpu.VMEM((D_MODEL // LANES, tm, LANES), _F32)],
        compiler_params=pltpu.CompilerParams(dimension_semantics=("arbitrary",),
                                             vmem_limit_bytes=VMEM_LIMIT_BYTES),
        name="mix_lru_s5",
    )(h, *consts)


def _ret_kernel(q_ref, k_ref, v_ref, g_ref, cos_ref, sin_ref, gn_ref, o_ref,
                state_ref, decay_ref, qdec_ref, kdec_ref):
    C = RET_CHUNK
    Dh = RET_HEAD_DIM
    log_gammas = [math.log1p(-(2.0 ** (-5.0 - h))) for h in range(RET_HEADS)]

    @pl.when((pl.program_id(0) == 0) & (pl.program_id(1) == 0))
    def _tables():
        row = lax.broadcasted_iota(jnp.int32, (C, C), 0)
        col = lax.broadcasted_iota(jnp.int32, (C, C), 1)
        rel = (row - col).astype(_F32)
        idx = lax.broadcasted_iota(jnp.int32, (C, Dh), 0).astype(_F32)
        for h in range(RET_HEADS):
            lg = log_gammas[h]
            decay_ref[h] = jnp.where(rel >= 0.0, jnp.exp(lg * jnp.maximum(rel, 0.0)), 0.0) * (Dh ** -0.5)
            qdec_ref[h] = jnp.exp(lg * (idx + 1.0))
            kdec_ref[h] = jnp.exp(lg * (C - 1.0 - idx)) * (Dh ** -0.5)

    @pl.when(pl.program_id(1) == 0)
    def _init():
        state_ref[...] = jnp.zeros_like(state_ref)

    cos2 = cos_ref[0]
    sin2 = sin_ref[0]

    def rot(t):
        return t * cos2 + pltpu.roll(t, Dh // 2, 1) * sin2

    for h in range(RET_HEADS):
        hs = slice(h * Dh, (h + 1) * Dh)
        qh = rot(q_ref[:, hs])
        kh = rot(k_ref[:, hs])
        vb = v_ref[:, hs].astype(_BF16)
        scores = lax.dot_general(qh.astype(_BF16), kh.astype(_BF16), (((1,), (1,)), ((), ())),
                                 preferred_element_type=_F32) * decay_ref[h]
        intra = _dot(scores.astype(_BF16), vb)
        state = state_ref[h]
        cross = _dot((qh * qdec_ref[h]).astype(_BF16), state.astype(_BF16))
        kv = lax.dot_general((kh * kdec_ref[h]).astype(_BF16), vb, (((0,), (0,)), ((), ())),
                             preferred_element_type=_F32)
        state_ref[h] = math.exp(log_gammas[h] * C) * state + kv
        o = intra + cross
        mu = jnp.mean(o, axis=-1, keepdims=True)
        oc = o - mu
        var = jnp.mean(oc * oc, axis=-1, keepdims=True)
        on = oc * lax.rsqrt(var + NORM_EPS) * gn_ref[:, hs]
        g = g_ref[:, hs]
        o_ref[:, hs] = (on * (g * _sigmoid(g))).astype(o_ref.dtype)


def _ret_call(qkvg, cos2, sin2, gn_gain):
    C = RET_CHUNK

    def part(j):
        return pl.BlockSpec((None, C, RET_WIDTH), lambda b, n, j=j: (b, n, j))

    return pl.pallas_call(
        _ret_kernel,
        grid=(BATCH, SEQ // C),
        in_specs=[part(0), part(1), part(2), part(3),
                  pl.BlockSpec((1, C, RET_HEAD_DIM), lambda b, n: (b, n, 0)),
                  pl.BlockSpec((1, C, RET_HEAD_DIM), lambda b, n: (b, n, 0)),
                  pl.BlockSpec((1, RET_WIDTH), lambda b, n: (0, 0))],
        out_specs=pl.BlockSpec((None, C, RET_WIDTH), lambda b, n: (b, n, 0)),
        out_shape=jax.ShapeDtypeStruct((BATCH, SEQ, RET_WIDTH), _F32),
        scratch_shapes=[pltpu.VMEM((RET_HEADS, RET_HEAD_DIM, RET_HEAD_DIM), _F32),
                        pltpu.VMEM((RET_HEADS, C, C), _F32),
                        pltpu.VMEM((RET_HEADS, C, RET_HEAD_DIM), _F32),
                        pltpu.VMEM((RET_HEADS, C, RET_HEAD_DIM), _F32)],
        compiler_params=pltpu.CompilerParams(dimension_semantics=("arbitrary", "arbitrary"),
                                             vmem_limit_bytes=VMEM_LIMIT_BYTES),
        name="retention",
    )(qkvg, qkvg, qkvg, qkvg, cos2, sin2, gn_gain)


def _ffn_kernel(h_ref, ya_ref, yr_ref, wo_a_ref, wo_r_ref, gain_ref, w_up_ref, convw_ref, convb_ref,
                w_down_ref, fin_ref, o_ref, acc, carry, yslab, hslab, *, batch_major_in, last):
    tm = FFN_TILE_ROWS
    fc = FFN_CHUNK

    @pl.when(pl.program_id(0) == 0)
    def _init():
        carry[...] = jnp.zeros_like(carry)

    h = _to_time_major(h_ref, hslab, FFN_TILE_STEPS) if batch_major_in else h_ref[...]
    yr = _to_time_major(yr_ref, yslab, FFN_TILE_STEPS).astype(_BF16)
    h1 = _dot(ya_ref[...], wo_a_ref[...]) + _dot(yr, wo_r_ref[...]) + h
    acc[...] = h1
    xn = _rmsnorm(h1, gain_ref[...]).astype(_BF16)
    def up_proj(j):
        return (_dot(xn, w_up_ref[:, j * fc:(j + 1) * fc]),
                _dot(xn, w_up_ref[:, D_FF + j * fc:D_FF + (j + 1) * fc]))

    n_chunks = D_FF // fc
    ups = up_proj(0)
    for j in range(n_chunks):
        nxt = up_proj(j + 1) if j + 1 < n_chunks else None
        vs = slice(j * fc, (j + 1) * fc)
        gs = slice(D_FF + j * fc, D_FF + (j + 1) * fc)
        upv, upg = ups
        cv = _causal_conv(upv, carry[:, vs], convw_ref, convb_ref[:, vs], vs)
        cg = _causal_conv(upg, carry[:, gs], convw_ref, convb_ref[:, gs], gs)
        carry[:, vs] = upv[tm - (FFN_CONV - 1) * BATCH:]
        carry[:, gs] = upg[tm - (FFN_CONV - 1) * BATCH:]
        act = (_gelu(cg) * cv).astype(_BF16)
        acc[...] += _dot(act, w_down_ref[vs, :])
        ups = nxt
    if last:
        out = _rmsnorm(acc[...], fin_ref[...])
        for s in range(D_MODEL // LANES):
            hslab[s] = out[:, s * LANES:(s + 1) * LANES]
        _from_time_major(hslab, o_ref, FFN_TILE_STEPS)
    else:
        o_ref[...] = acc[...]


def _ffn_call(h, ya, yr, wo_a, wo_r, gain, w_up, convw, convb, w_down, fin_gain, batch_major_in, last):
    tm = FFN_TILE_ROWS
    consts = [wo_a, wo_r, gain, w_up, convw, convb, w_down, fin_gain]
    out_shape = (BATCH, SEQ, D_MODEL) if last else (ROWS, D_MODEL)
    return pl.pallas_call(
        functools.partial(_ffn_kernel, batch_major_in=batch_major_in, last=last),
        grid=(ROWS // tm,),
        in_specs=[_stream_spec(batch_major_in, FFN_TILE_STEPS, D_MODEL),
                  pl.BlockSpec((tm, MIX_A_WIDTH), lambda i: (i, 0)),
                  _stream_spec(True, FFN_TILE_STEPS, RET_WIDTH)]
                 + [_const_spec(c.shape) for c in consts],
        out_specs=_stream_spec(last, FFN_TILE_STEPS, D_MODEL),
        out_shape=jax.ShapeDtypeStruct(out_shape, _F32),
        scratch_shapes=[pltpu.VMEM((tm, D_MODEL), _F32),
                        pltpu.VMEM(((FFN_CONV - 1) * BATCH, 2 * D_FF), _F32),
                        pltpu.VMEM((RET_WIDTH // LANES, tm, LANES), _F32),
                        pltpu.VMEM((D_MODEL // LANES, tm, LANES), _F32)],
        compiler_params=pltpu.CompilerParams(dimension_semantics=("arbitrary",),
                                             vmem_limit_bytes=VMEM_LIMIT_BYTES),
        name="outproj_ffn",
    )(h, ya, yr, *consts)


def _block_diag(w):
    hh, ii, jj = w.shape
    eye = jnp.eye(hh, dtype=w.dtype)
    return jnp.einsum('hij,hg->higj', w, eye).reshape(hh * ii, hh * jj)


def _pack_gate(w):
    per = MXU_WIDTH // (LRU_WIDTH // LRU_BLOCKS)
    return jnp.stack([_block_diag(w[p * per:(p + 1) * per])
                      for p in range(LRU_BLOCKS // per)]).astype(_BF16)


def _s5_params(lam_re, lam_im, log_dt, b_re, b_im, c_re, c_im):
    dt = jnp.exp(log_dt.astype(_F32))[:, None]
    lr, li = lam_re.astype(_F32), lam_im.astype(_F32)
    mag = jnp.exp(lr * dt)
    abar_re, abar_im = mag * jnp.cos(li * dt), mag * jnp.sin(li * dt)
    den = lr * lr + li * li
    nr, ni = abar_re - 1.0, abar_im
    coef_re = ((nr * lr + ni * li) / den)[..., None]
    coef_im = ((ni * lr - nr * li) / den)[..., None]
    br, bi = b_re.astype(_F32), b_im.astype(_F32)
    bbar_re = coef_re * br - coef_im * bi
    bbar_im = coef_re * bi + coef_im * br

    def pack_b(bbar):
        dense = _block_diag(bbar.transpose(0, 2, 1))
        tiles = [dense[LANES * (j // 2):LANES * (j // 2 + 1), MXU_WIDTH * j:MXU_WIDTH * (j + 1)]
                 for j in range(S5_STATES // MXU_WIDTH)]
        return jnp.stack(tiles).astype(_BF16)

    def pack_c(c):
        dense = _block_diag(c.astype(_F32).transpose(0, 2, 1))
        tiles = [dense[512 * s:512 * (s + 1), LANES * s:LANES * (s + 1)]
                 for s in range(S5_WIDTH // LANES)]
        return jnp.stack(tiles).astype(_BF16)

    return (abar_re.reshape(1, S5_STATES), abar_im.reshape(1, S5_STATES),
            pack_b(bbar_re), pack_b(bbar_im), pack_c(c_re), pack_c(c_im))


def kernel(x, positions, norm_mix, w_in, lru_conv_w, lru_conv_b, lru_wa, lru_ba, lru_wx, lru_bx, lru_lambda, lru_norm, s5_lambda_re, s5_lambda_im, s5_log_dt, s5_b_re, s5_b_im, s5_c_re, s5_c_im, s5_d, s5_w_glu, s5_b_glu, s5_norm, ret_norm, w_out, norm_ffn, w_up, ffn_conv_w, ffn_conv_b, w_down, norm_final):
    cos2, sin2 = _rotary_tables(positions)
    h = x
    row = lambda v: v.reshape(1, -1).astype(_F32)
    for l in range(DEPTH):
        first, last = l == 0, l == DEPTH - 1
        lru = (lru_conv_w[l], row(lru_conv_b[l]),
               _pack_gate(lru_wa[l]), row(lru_ba[l]), _pack_gate(lru_wx[l]), row(lru_bx[l]),
               row(jax.nn.softplus(-lru_lambda[l].astype(_F32))), row(lru_norm[l]))
        s5 = (*_s5_params(s5_lambda_re[l], s5_lambda_im[l], s5_log_dt[l], s5_b_re[l], s5_b_im[l],
                          s5_c_re[l], s5_c_im[l]),
              row(s5_d[l]), s5_w_glu[l].astype(_BF16), row(s5_b_glu[l]), row(s5_norm[l]))
        ya, qkvg = _mix_call(h, row(norm_mix[l]), w_in[l].astype(_BF16), lru, s5, batch_major_in=first)
        yr = _ret_call(qkvg, cos2, sin2, row(ret_norm[l]))
        wo = w_out[l].astype(_BF16)
        h = _ffn_call(h, ya, yr, wo[:MIX_A_WIDTH], wo[MIX_A_WIDTH:], row(norm_ffn[l]),
                      w_up[l].astype(_BF16), ffn_conv_w[l], row(ffn_conv_b[l]),
                      w_down[l].astype(_BF16), row(norm_final), batch_major_in=first, last=last)
    return h
```

```python
import functools
import math

import jax
import jax.numpy as jnp
from jax import lax
from jax.experimental import pallas as pl
from jax.experimental.pallas import tpu as pltpu

D_MODEL = 1024
BATCH = 8
SEQ = 2048
DEPTH = 2
ROWS = BATCH * SEQ

LRU_WIDTH = 512
LRU_BLOCKS = 8
LRU_CONV = 4
LRU_C = 8.0
S5_WIDTH = 512
S5_GROUP = 16
S5_GROUPS = 32
S5_STATE = 64
S5_STATES = S5_GROUPS * S5_STATE
RET_HEADS = 4
RET_HEAD_DIM = 128
RET_WIDTH = 512
ROPE_BASE = 10000.0
MIX_A_WIDTH = LRU_WIDTH + S5_WIDTH
PROJ_A_WIDTH = 2 * LRU_WIDTH + S5_WIDTH
QKVG_WIDTH = 4 * RET_WIDTH
IN_WIDTH = PROJ_A_WIDTH + QKVG_WIDTH
D_FF = 3 * D_MODEL
FFN_CONV = 3
NORM_EPS = 1e-6

SUBLANES = 8
LANES = 128
MXU_WIDTH = 256
VMEM_LIMIT_BYTES = 56 * 1024 * 1024

MIX_TILE_STEPS = 64
MIX_TILE_ROWS = MIX_TILE_STEPS * BATCH
FFN_TILE_STEPS = 64
FFN_TILE_ROWS = FFN_TILE_STEPS * BATCH
FFN_CHUNK = 1024
RET_CHUNK = 256
S5_SCAN_LANES = 512
S5_SCAN_BLOCK = 32

_F32 = jnp.float32
_BF16 = jnp.bfloat16


def _gelu(x):
    return 0.5 * x * (1.0 + jnp.tanh(0.7978845608028654 * (x + 0.044715 * (x * x * x))))


def _sigmoid(x):
    return 1.0 / (1.0 + jnp.exp(-x))


def _rmsnorm(x, gain):
    return x * lax.rsqrt(jnp.mean(x * x, axis=-1, keepdims=True) + NORM_EPS) * gain


def _dot(a, b):
    return jnp.dot(a, b, preferred_element_type=_F32)


def _to_time_major(x_ref, slab, steps):
    n = x_ref.shape[-1] // LANES
    for b in range(BATCH):
        for s in range(n):
            slab[s, pl.ds(b, steps, stride=BATCH), :] = x_ref[b, :, s * LANES:(s + 1) * LANES]
    return jnp.concatenate([slab[s] for s in range(n)], axis=-1)


def _from_time_major(slab, o_ref, steps):
    n = o_ref.shape[-1] // LANES
    for b in range(BATCH):
        for s in range(n):
            o_ref[b, :, s * LANES:(s + 1) * LANES] = (
                slab[s, pl.ds(b, steps, stride=BATCH), :].astype(o_ref.dtype))


def _causal_conv(x, hist, w_ref, bias, cols):
    taps = w_ref.shape[0]
    tm = x.shape[0]
    y = bias
    for k in range(taps):
        back = (taps - 1 - k) * BATCH
        xs = x if back == 0 else jnp.concatenate([hist[hist.shape[0] - back:], x[:tm - back]], axis=0)
        y = y + w_ref[k:k + 1, cols] * xs
    return y


def _rotary_kernel(pos_ref, inv_ref, sign_ref, cos_ref, sin_ref):
    ang = pos_ref[0].astype(_F32) * inv_ref[...]
    cos_ref[0] = jnp.cos(ang)
    sin_ref[0] = jnp.sin(ang) * sign_ref[...]


def _rotary_tables(positions):
    half = RET_HEAD_DIM // 2
    inv = ROPE_BASE ** (-jnp.arange(half, dtype=_F32) * 2.0 / RET_HEAD_DIM)
    inv2 = jnp.concatenate([inv, inv])[None, :]
    sign = jnp.concatenate([-jnp.ones((half,), _F32), jnp.ones((half,), _F32)])[None, :]
    pos3 = positions.reshape(BATCH, SEQ, 1)
    out = jax.ShapeDtypeStruct((BATCH, SEQ, RET_HEAD_DIM), _F32)
    return pl.pallas_call(
        _rotary_kernel,
        grid=(BATCH,),
        in_specs=[pl.BlockSpec((1, SEQ, 1), lambda b: (b, 0, 0)),
                  pl.BlockSpec((1, RET_HEAD_DIM), lambda b: (0, 0)),
                  pl.BlockSpec((1, RET_HEAD_DIM), lambda b: (0, 0))],
        out_specs=[pl.BlockSpec((1, SEQ, RET_HEAD_DIM), lambda b: (b, 0, 0)),
                   pl.BlockSpec((1, SEQ, RET_HEAD_DIM), lambda b: (b, 0, 0))],
        out_shape=[out, out],
        name="rotary_tables",
    )(pos3, inv2, sign)


def _mix_kernel(h_ref, gain_ref, w_in_ref,
                convw_ref, convb_ref, wa_ref, ba_ref, wx_ref, bx_ref, sp_ref, lrun_ref,
                are_ref, aim_ref, wbre_ref, wbim_ref, cre_ref, cim_ref, d_ref,
                wglu_ref, bglu_ref, s5n_ref,
                ya_ref, qkvg_ref,
                conv_hist, gate_s, u_s, la_s, lb_s, lru_state, sr, si, s5_state_r, s5_state_i, qslab, hslab,
                *, batch_major_in):
    tm = MIX_TILE_ROWS

    @pl.when(pl.program_id(0) == 0)
    def _init():
        conv_hist[...] = jnp.zeros_like(conv_hist)
        lru_state[...] = jnp.zeros_like(lru_state)
        s5_state_r[...] = jnp.zeros_like(s5_state_r)
        s5_state_i[...] = jnp.zeros_like(s5_state_i)

    h = _to_time_major(h_ref, hslab, MIX_TILE_STEPS) if batch_major_in else h_ref[...]
    xn = _rmsnorm(h, gain_ref[...]).astype(_BF16)

    u_s[...] = _dot(xn, w_in_ref[:, 2 * LRU_WIDTH:PROJ_A_WIDTH])
    ub = u_s[...].astype(_BF16)
    for j in range(S5_STATES // MXU_WIDTH):
        slab = ub[:, LANES * (j // 2):LANES * (j // 2 + 1)]
        sr[:, MXU_WIDTH * j:MXU_WIDTH * (j + 1)] = _dot(slab, wbre_ref[j])
        si[:, MXU_WIDTH * j:MXU_WIDTH * (j + 1)] = _dot(slab, wbim_ref[j])

    lru_parts = [None] * (LRU_WIDTH // MXU_WIDTH)

    def project_lru(p):
        lru_parts[p] = _dot(xn, w_in_ref[:, MXU_WIDTH * p:MXU_WIDTH * (p + 1)])

    def project_gate(p):
        cols = slice(MXU_WIDTH * p, MXU_WIDTH * (p + 1))
        gate_s[:, cols] = _dot(xn, w_in_ref[:, LRU_WIDTH + MXU_WIDTH * p:LRU_WIDTH + MXU_WIDTH * (p + 1)])

    def project_qkvg(c):
        res = _dot(xn, w_in_ref[:, PROJ_A_WIDTH + MXU_WIDTH * c:PROJ_A_WIDTH + MXU_WIDTH * (c + 1)])
        for s in range(MXU_WIDTH // LANES):
            qslab[c * (MXU_WIDTH // LANES) + s] = res[:, s * LANES:(s + 1) * LANES]

    n_qkvg = QKVG_WIDTH // MXU_WIDTH
    jobs = ([functools.partial(project_lru, p) for p in range(LRU_WIDTH // MXU_WIDTH)]
            + [functools.partial(project_gate, p) for p in range(LRU_WIDTH // MXU_WIDTH)]
            + [functools.partial(project_qkvg, c) for c in range(n_qkvg // 2)])
    jobs = iter(jobs)

    for sg in range(S5_STATES // S5_SCAN_LANES):
        sl = slice(sg * S5_SCAN_LANES, (sg + 1) * S5_SCAN_LANES)
        ar = jnp.broadcast_to(are_ref[:, sl], (BATCH, S5_SCAN_LANES))
        ai = jnp.broadcast_to(aim_ref[:, sl], (BATCH, S5_SCAN_LANES))
        xr = s5_state_r[:, sl]
        xi = s5_state_i[:, sl]
        for t in range(MIX_TILE_STEPS):
            rows = slice(t * BATCH, (t + 1) * BATCH)
            xr, xi = (ar * xr - ai * xi + sr[rows, sl], ar * xi + ai * xr + si[rows, sl])
            sr[rows, sl] = xr
            si[rows, sl] = xi
            if (t + 1) % S5_SCAN_BLOCK == 0:
                next(jobs, lambda: None)()
        s5_state_r[:, sl] = xr
        s5_state_i[:, sl] = xi
    for job in jobs:
        job()

    lru_x = jnp.concatenate(lru_parts, axis=-1)
    xc = _causal_conv(lru_x, conv_hist[...], convw_ref, convb_ref[...], slice(None))
    conv_hist[...] = lru_x[tm - (LRU_CONV - 1) * BATCH:]
    xcb = xc.astype(_BF16)
    halves = range(LRU_WIDTH // MXU_WIDTH)
    pre_r = jnp.concatenate([_dot(xcb[:, MXU_WIDTH * p:MXU_WIDTH * (p + 1)], wa_ref[p]) for p in halves], axis=-1)
    pre_i = jnp.concatenate([_dot(xcb[:, MXU_WIDTH * p:MXU_WIDTH * (p + 1)], wx_ref[p]) for p in halves], axis=-1)
    r = _sigmoid(pre_r + ba_ref[...])
    i = _sigmoid(pre_i + bx_ref[...])
    log_a = (-LRU_C) * r * sp_ref[...]
    a = jnp.exp(log_a)
    la_s[...] = a
    z = -jnp.tanh(log_a) * (1.0 + a * a)
    lb_s[...] = jnp.where(z == 0.0, 0.0, z * lax.rsqrt(z)) * (i * xc)

    ys = []
    for s in range(S5_WIDTH // LANES):
        ksl = slice(512 * s, 512 * (s + 1))
        ys.append(_dot(sr[:, ksl].astype(_BF16), cre_ref[s])
                  - _dot(si[:, ksl].astype(_BF16), cim_ref[s]))
    y = jnp.concatenate(ys, axis=-1) + d_ref[...] * u_s[...]
    z = _gelu(y)
    glu = _dot(z.astype(_BF16), wglu_ref[...])

    for c in range(n_qkvg // 2, n_qkvg):
        project_qkvg(c)

    hs = lru_state[...]
    for t in range(MIX_TILE_STEPS):
        rows = slice(t * BATCH, (t + 1) * BATCH)
        hs = la_s[rows, :] * hs + lb_s[rows, :]
        lb_s[rows, :] = hs
    lru_state[...] = hs
    y_lru = lb_s[...] * _gelu(gate_s[...])
    ya_ref[:, 0:LRU_WIDTH] = _rmsnorm(y_lru, lrun_ref[...]).astype(ya_ref.dtype)

    out = z * _sigmoid(glu + bglu_ref[...])
    ya_ref[:, LRU_WIDTH:MIX_A_WIDTH] = _rmsnorm(out, s5n_ref[...]).astype(ya_ref.dtype)
    _from_time_major(qslab, qkvg_ref, MIX_TILE_STEPS)


def _layer_spec(arr, l):
    nd = arr.ndim
    return pl.BlockSpec((None,) + arr.shape[1:], lambda *_: (l,) + (0,) * (nd - 1),
                        pipeline_mode=pl.Buffered(1))


def _stream_spec(batch_major, steps, width):
    if batch_major:
        return pl.BlockSpec((BATCH, steps, width), lambda i: (0, i, 0))
    return pl.BlockSpec((steps * BATCH, width), lambda i: (i, 0))


def _mix_call(h, consts, l, batch_major_in):
    tm = MIX_TILE_ROWS
    return pl.pallas_call(
        functools.partial(_mix_kernel, batch_major_in=batch_major_in),
        grid=(ROWS // tm,),
        in_specs=[_stream_spec(batch_major_in, MIX_TILE_STEPS, D_MODEL)] + [_layer_spec(c, l) for c in consts],
        out_specs=[pl.BlockSpec((tm, MIX_A_WIDTH), lambda i: (i, 0)),
                   _stream_spec(True, MIX_TILE_STEPS, QKVG_WIDTH)],
        out_shape=[jax.ShapeDtypeStruct((ROWS, MIX_A_WIDTH), _BF16),
                   jax.ShapeDtypeStruct((BATCH, SEQ, QKVG_WIDTH), _BF16)],
        scratch_shapes=[pltpu.VMEM(((LRU_CONV - 1) * BATCH, LRU_WIDTH), _F32),
                        pltpu.VMEM((tm, LRU_WIDTH), _F32),
                        pltpu.VMEM((tm, S5_WIDTH), _F32),
                        pltpu.VMEM((tm, LRU_WIDTH), _F32),
                        pltpu.VMEM((tm, LRU_WIDTH), _F32),
                        pltpu.VMEM((BATCH, LRU_WIDTH), _F32),
                        pltpu.VMEM((tm, S5_STATES), _F32),
                        pltpu.VMEM((tm, S5_STATES), _F32),
                        pltpu.VMEM((BATCH, S5_STATES), _F32),
                        pltpu.VMEM((BATCH, S5_STATES), _F32),
                        pltpu.VMEM((QKVG_WIDTH // LANES, tm, LANES), _F32),
                        pltpu.VMEM((D_MODEL // LANES, tm, LANES), _F32)],
        compiler_params=pltpu.CompilerParams(dimension_semantics=("arbitrary",),
                                             vmem_limit_bytes=VMEM_LIMIT_BYTES),
        name="mix_lru_s5",
    )(h, *consts)


def _ret_kernel(qkvg_ref, cos_ref, sin_ref, gn_ref, o_ref,
                state_ref, decay_ref, qdec_ref, kdec_ref):
    C = RET_CHUNK
    Dh = RET_HEAD_DIM
    log_gammas = [math.log1p(-(2.0 ** (-5.0 - h))) for h in range(RET_HEADS)]

    @pl.when((pl.program_id(0) == 0) & (pl.program_id(1) == 0))
    def _tables():
        row = lax.broadcasted_iota(jnp.int32, (C, C), 0)
        col = lax.broadcasted_iota(jnp.int32, (C, C), 1)
        rel = (row - col).astype(_F32)
        idx = lax.broadcasted_iota(jnp.int32, (C, Dh), 0).astype(_F32)
        for h in range(RET_HEADS):
            lg = log_gammas[h]
            decay_ref[h] = jnp.where(rel >= 0.0, jnp.exp(lg * jnp.maximum(rel, 0.0)), 0.0) * (Dh ** -0.5)
            qdec_ref[h] = jnp.exp(lg * (idx + 1.0))
            kdec_ref[h] = jnp.exp(lg * (C - 1.0 - idx)) * (Dh ** -0.5)

    @pl.when(pl.program_id(1) == 0)
    def _init():
        state_ref[...] = jnp.zeros_like(state_ref)

    cos2 = cos_ref[0]
    sin2 = sin_ref[0]

    def rot(t):
        return t * cos2 + pltpu.roll(t, Dh // 2, 1) * sin2

    for h in range(RET_HEADS):
        hs = slice(h * Dh, (h + 1) * Dh)
        qh = rot(qkvg_ref[:, h * Dh:(h + 1) * Dh].astype(_F32))
        kh = rot(qkvg_ref[:, RET_WIDTH + h * Dh:RET_WIDTH + (h + 1) * Dh].astype(_F32))
        vb = qkvg_ref[:, 2 * RET_WIDTH + h * Dh:2 * RET_WIDTH + (h + 1) * Dh]
        scores = lax.dot_general(qh.astype(_BF16), kh.astype(_BF16), (((1,), (1,)), ((), ())),
                                 preferred_element_type=_F32) * decay_ref[h]
        intra = _dot(scores.astype(_BF16), vb)
        state = state_ref[h]
        cross = _dot((qh * qdec_ref[h]).astype(_BF16), state.astype(_BF16))
        kv = lax.dot_general((kh * kdec_ref[h]).astype(_BF16), vb, (((0,), (0,)), ((), ())),
                             preferred_element_type=_F32)
        state_ref[h] = math.exp(log_gammas[h] * C) * state + kv
        o = intra + cross
        mu = jnp.mean(o, axis=-1, keepdims=True)
        oc = o - mu
        var = jnp.mean(oc * oc, axis=-1, keepdims=True)
        on = oc * lax.rsqrt(var + NORM_EPS) * gn_ref[:, hs]
        g = qkvg_ref[:, 3 * RET_WIDTH + h * Dh:3 * RET_WIDTH + (h + 1) * Dh].astype(_F32)
        o_ref[:, hs] = (on * (g * _sigmoid(g))).astype(o_ref.dtype)


def _ret_call(qkvg, cos2, sin2, gn_gain, l):
    C = RET_CHUNK
    return pl.pallas_call(
        _ret_kernel,
        grid=(BATCH, SEQ // C),
        in_specs=[pl.BlockSpec((None, C, QKVG_WIDTH), lambda b, n: (b, n, 0)),
                  pl.BlockSpec((1, C, RET_HEAD_DIM), lambda b, n: (b, n, 0)),
                  pl.BlockSpec((1, C, RET_HEAD_DIM), lambda b, n: (b, n, 0)),
                  pl.BlockSpec((None, 1, RET_WIDTH), lambda b, n: (l, 0, 0))],
        out_specs=pl.BlockSpec((None, C, RET_WIDTH), lambda b, n: (b, n, 0)),
        out_shape=jax.ShapeDtypeStruct((BATCH, SEQ, RET_WIDTH), _F32),
        scratch_shapes=[pltpu.VMEM((RET_HEADS, RET_HEAD_DIM, RET_HEAD_DIM), _F32),
                        pltpu.VMEM((RET_HEADS, C, C), _F32),
                        pltpu.VMEM((RET_HEADS, C, RET_HEAD_DIM), _F32),
                        pltpu.VMEM((RET_HEADS, C, RET_HEAD_DIM), _F32)],
        compiler_params=pltpu.CompilerParams(dimension_semantics=("arbitrary", "arbitrary"),
                                             vmem_limit_bytes=VMEM_LIMIT_BYTES),
        name="retention",
    )(qkvg, cos2, sin2, gn_gain)


def _ffn_kernel(h_ref, ya_ref, yr_ref, wo_ref, gain_ref, w_up_ref, convw_ref, convb_ref,
                w_down_ref, fin_ref, o_ref, acc, carry, yslab, hslab, *, batch_major_in, last):
    tm = FFN_TILE_ROWS
    fc = FFN_CHUNK

    @pl.when(pl.program_id(0) == 0)
    def _init():
        carry[...] = jnp.zeros_like(carry)

    h = _to_time_major(h_ref, hslab, FFN_TILE_STEPS) if batch_major_in else h_ref[...]
    yr = _to_time_major(yr_ref, yslab, FFN_TILE_STEPS).astype(_BF16)
    h1 = _dot(ya_ref[...], wo_ref[0:MIX_A_WIDTH, :]) + _dot(yr, wo_ref[MIX_A_WIDTH:, :]) + h
    acc[...] = h1
    xn = _rmsnorm(h1, gain_ref[...]).astype(_BF16)
    def up_proj(j):
        return (_dot(xn, w_up_ref[:, j * fc:(j + 1) * fc]),
                _dot(xn, w_up_ref[:, D_FF + j * fc:D_FF + (j + 1) * fc]))

    n_chunks = D_FF // fc
    ups = up_proj(0)
    for j in range(n_chunks):
        nxt = up_proj(j + 1) if j + 1 < n_chunks else None
        vs = slice(j * fc, (j + 1) * fc)
        gs = slice(D_FF + j * fc, D_FF + (j + 1) * fc)
        upv, upg = ups
        cv = _causal_conv(upv, carry[:, vs], convw_ref, convb_ref[:, vs], vs)
        cg = _causal_conv(upg, carry[:, gs], convw_ref, convb_ref[:, gs], gs)
        carry[:, vs] = upv[tm - (FFN_CONV - 1) * BATCH:]
        carry[:, gs] = upg[tm - (FFN_CONV - 1) * BATCH:]
        act = (_gelu(cg) * cv).astype(_BF16)
        acc[...] += _dot(act, w_down_ref[vs, :])
        ups = nxt
    if last:
        out = _rmsnorm(acc[...], fin_ref[...])
        for s in range(D_MODEL // LANES):
            hslab[s] = out[:, s * LANES:(s + 1) * LANES]
        _from_time_major(hslab, o_ref, FFN_TILE_STEPS)
    else:
        o_ref[...] = acc[...]


def _ffn_call(h, ya, yr, consts, fin_gain, l, batch_major_in, last):
    tm = FFN_TILE_ROWS
    out_shape = (BATCH, SEQ, D_MODEL) if last else (ROWS, D_MODEL)
    return pl.pallas_call(
        functools.partial(_ffn_kernel, batch_major_in=batch_major_in, last=last),
        grid=(ROWS // tm,),
        in_specs=[_stream_spec(batch_major_in, FFN_TILE_STEPS, D_MODEL),
                  pl.BlockSpec((tm, MIX_A_WIDTH), lambda i: (i, 0)),
                  _stream_spec(True, FFN_TILE_STEPS, RET_WIDTH)]
                 + [_layer_spec(c, l) for c in consts] + [_layer_spec(fin_gain, 0)],
        out_specs=_stream_spec(last, FFN_TILE_STEPS, D_MODEL),
        out_shape=jax.ShapeDtypeStruct(out_shape, _F32),
        scratch_shapes=[pltpu.VMEM((tm, D_MODEL), _F32),
                        pltpu.VMEM(((FFN_CONV - 1) * BATCH, 2 * D_FF), _F32),
                        pltpu.VMEM((RET_WIDTH // LANES, tm, LANES), _F32),
                        pltpu.VMEM((D_MODEL // LANES, tm, LANES), _F32)],
        compiler_params=pltpu.CompilerParams(dimension_semantics=("arbitrary",),
                                             vmem_limit_bytes=VMEM_LIMIT_BYTES),
        name="outproj_ffn",
    )(h, ya, yr, *consts, fin_gain)


def _pack_gate(w):
    blk = LRU_WIDTH // LRU_BLOCKS
    per = MXU_WIDTH // blk
    w5 = w.astype(_F32).reshape(DEPTH, LRU_BLOCKS // per, per, blk, blk)
    t = jnp.einsum('lphij,hg->lphigj', w5, jnp.eye(per, dtype=_F32))
    return t.reshape(DEPTH, LRU_BLOCKS // per, MXU_WIDTH, MXU_WIDTH).astype(_BF16)


def _s5_params(lam_re, lam_im, log_dt, b_re, b_im, c_re, c_im):
    dt = jnp.exp(log_dt.astype(_F32))[..., None]
    lr, li = lam_re.astype(_F32), lam_im.astype(_F32)
    mag = jnp.exp(lr * dt)
    abar_re, abar_im = mag * jnp.cos(li * dt), mag * jnp.sin(li * dt)
    den = lr * lr + li * li
    nr, ni = abar_re - 1.0, abar_im
    coef_re = ((nr * lr + ni * li) / den)[..., None]
    coef_im = ((ni * lr - nr * li) / den)[..., None]
    br, bi = b_re.astype(_F32), b_im.astype(_F32)
    bbar_re = coef_re * br - coef_im * bi
    bbar_im = coef_re * bi + coef_im * br

    n_bt = S5_STATES // MXU_WIDTH
    g_out = MXU_WIDTH // S5_STATE
    g_in = LANES // S5_GROUP
    sel = (jnp.arange(g_in)[None, :, None]
           == (g_out * (jnp.arange(n_bt) % 2))[:, None, None] + jnp.arange(g_out)[None, None, :]).astype(_F32)

    def pack_b(bbar):
        bb = bbar.reshape(DEPTH, n_bt, g_out, S5_STATE, S5_GROUP)
        t = jnp.einsum('ljopc,jio->ljicop', bb, sel)
        return t.reshape(DEPTH, n_bt, LANES, MXU_WIDTH).astype(_BF16)

    def pack_c(c):
        n_sl = S5_WIDTH // LANES
        cc = c.astype(_F32).reshape(DEPTH, n_sl, g_in, S5_GROUP, S5_STATE)
        t = jnp.einsum('lsicp,io->lsipoc', cc, jnp.eye(g_in, dtype=_F32))
        return t.reshape(DEPTH, n_sl, g_in * S5_STATE, LANES).astype(_BF16)

    return (abar_re.reshape(DEPTH, 1, S5_STATES), abar_im.reshape(DEPTH, 1, S5_STATES),
            pack_b(bbar_re), pack_b(bbar_im), pack_c(c_re), pack_c(c_im))


def kernel(x, positions, norm_mix, w_in, lru_conv_w, lru_conv_b, lru_wa, lru_ba, lru_wx, lru_bx, lru_lambda, lru_norm, s5_lambda_re, s5_lambda_im, s5_log_dt, s5_b_re, s5_b_im, s5_c_re, s5_c_im, s5_d, s5_w_glu, s5_b_glu, s5_norm, ret_norm, w_out, norm_ffn, w_up, ffn_conv_w, ffn_conv_b, w_down, norm_final):
    cos2, sin2 = _rotary_tables(positions)
    row = lambda v: v.reshape(v.shape[0], 1, -1).astype(_F32)
    mix_consts = [row(norm_mix), w_in.astype(_BF16),
                  lru_conv_w.astype(_F32), row(lru_conv_b),
                  _pack_gate(lru_wa), row(lru_ba), _pack_gate(lru_wx), row(lru_bx),
                  row(jax.nn.softplus(-lru_lambda.astype(_F32))), row(lru_norm),
                  *_s5_params(s5_lambda_re, s5_lambda_im, s5_log_dt, s5_b_re, s5_b_im, s5_c_re, s5_c_im),
                  row(s5_d), s5_w_glu.astype(_BF16), row(s5_b_glu), row(s5_norm)]
    ffn_consts = [w_out.astype(_BF16), row(norm_ffn), w_up.astype(_BF16),
                  ffn_conv_w.astype(_F32), row(ffn_conv_b), w_down.astype(_BF16)]
    gn_gain = row(ret_norm)
    fin_gain = norm_final.reshape(1, 1, D_MODEL).astype(_F32)
    h = x
    for l in range(DEPTH):
        first, last = l == 0, l == DEPTH - 1
        ya, qkvg = _mix_call(h, mix_consts, l, batch_major_in=first)
        yr = _ret_call(qkvg, cos2, sin2, gn_gain, l)
        h = _ffn_call(h, ya, yr, ffn_consts, fin_gain, l, batch_major_in=first, last=last)
    return h
```

```python
import functools
import math

import jax
import jax.numpy as jnp
from jax import lax
from jax.experimental import pallas as pl
from jax.experimental.pallas import tpu as pltpu

D_MODEL = 1024
BATCH = 8
SEQ = 2048
DEPTH = 2
ROWS = BATCH * SEQ

LRU_WIDTH = 512
LRU_BLOCKS = 8
LRU_CONV = 4
LRU_C = 8.0
S5_WIDTH = 512
S5_GROUP = 16
S5_GROUPS = 32
S5_STATE = 64
S5_STATES = S5_GROUPS * S5_STATE
RET_HEADS = 4
RET_HEAD_DIM = 128
RET_WIDTH = 512
ROPE_BASE = 10000.0
MIX_A_WIDTH = LRU_WIDTH + S5_WIDTH
PROJ_A_WIDTH = 2 * LRU_WIDTH + S5_WIDTH
QKVG_WIDTH = 4 * RET_WIDTH
IN_WIDTH = PROJ_A_WIDTH + QKVG_WIDTH
D_FF = 3 * D_MODEL
FFN_CONV = 3
NORM_EPS = 1e-6

SUBLANES = 8
LANES = 128
MXU_WIDTH = 256
VMEM_LIMIT_BYTES = 56 * 1024 * 1024

MIX_TILE_STEPS = 64
MIX_TILE_ROWS = MIX_TILE_STEPS * BATCH
FFN_TILE_STEPS = 64
FFN_TILE_ROWS = FFN_TILE_STEPS * BATCH
FFN_CHUNK = 1024
RET_CHUNK = 256
S5_SCAN_LANES = 512

_F32 = jnp.float32
_BF16 = jnp.bfloat16


def _gelu(x):
    return 0.5 * x * (1.0 + jnp.tanh(0.7978845608028654 * (x + 0.044715 * (x * x * x))))


def _sigmoid(x):
    return 1.0 / (1.0 + jnp.exp(-x))


def _rmsnorm(x, gain):
    return x * lax.rsqrt(jnp.mean(x * x, axis=-1, keepdims=True) + NORM_EPS) * gain


def _dot(a, b):
    return jnp.dot(a, b, preferred_element_type=_F32)


def _to_time_major(x_ref, slab, steps):
    n = x_ref.shape[-1] // LANES
    for b in range(BATCH):
        for s in range(n):
            slab[s, pl.ds(b, steps, stride=BATCH), :] = x_ref[b, :, s * LANES:(s + 1) * LANES]
    return jnp.concatenate([slab[s] for s in range(n)], axis=-1)


def _from_time_major(slab, o_ref, steps):
    n = o_ref.shape[-1] // LANES
    for b in range(BATCH):
        for s in range(n):
            o_ref[b, :, s * LANES:(s + 1) * LANES] = (
                slab[s, pl.ds(b, steps, stride=BATCH), :].astype(o_ref.dtype))


def _causal_conv(x, hist, w_ref, bias, cols):
    taps = w_ref.shape[0]
    tm = x.shape[0]
    y = bias
    for k in range(taps):
        back = (taps - 1 - k) * BATCH
        xs = x if back == 0 else jnp.concatenate([hist[hist.shape[0] - back:], x[:tm - back]], axis=0)
        y = y + w_ref[k:k + 1, cols] * xs
    return y


def _rotary_kernel(pos_ref, inv_ref, sign_ref, cos_ref, sin_ref):
    ang = pos_ref[0].astype(_F32) * inv_ref[...]
    cos_ref[0] = jnp.cos(ang)
    sin_ref[0] = jnp.sin(ang) * sign_ref[...]


def _rotary_tables(positions):
    half = RET_HEAD_DIM // 2
    inv = ROPE_BASE ** (-jnp.arange(half, dtype=_F32) * 2.0 / RET_HEAD_DIM)
    inv2 = jnp.concatenate([inv, inv])[None, :]
    sign = jnp.concatenate([-jnp.ones((half,), _F32), jnp.ones((half,), _F32)])[None, :]
    pos3 = positions.reshape(BATCH, SEQ, 1)
    out = jax.ShapeDtypeStruct((BATCH, SEQ, RET_HEAD_DIM), _F32)
    return pl.pallas_call(
        _rotary_kernel,
        grid=(BATCH,),
        in_specs=[pl.BlockSpec((1, SEQ, 1), lambda b: (b, 0, 0)),
                  pl.BlockSpec((1, RET_HEAD_DIM), lambda b: (0, 0)),
                  pl.BlockSpec((1, RET_HEAD_DIM), lambda b: (0, 0))],
        out_specs=[pl.BlockSpec((1, SEQ, RET_HEAD_DIM), lambda b: (b, 0, 0)),
                   pl.BlockSpec((1, SEQ, RET_HEAD_DIM), lambda b: (b, 0, 0))],
        out_shape=[out, out],
        name="rotary_tables",
    )(pos3, inv2, sign)


def _mix_kernel(h_ref, gain_ref, w_in_ref,
                convw_ref, convb_ref, wa_ref, ba_ref, wx_ref, bx_ref, sp_ref, lrun_ref,
                are_ref, aim_ref, wbre_ref, wbim_ref, cre_ref, cim_ref, d_ref,
                wglu_ref, bglu_ref, s5n_ref,
                ya_ref, qkvg_ref,
                conv_hist, gate_s, u_s, la_s, lb_s, lru_state, sr, si, s5_state_r, s5_state_i, qslab, hslab,
                *, batch_major_in):
    tm = MIX_TILE_ROWS

    @pl.when(pl.program_id(0) == 0)
    def _init():
        conv_hist[...] = jnp.zeros_like(conv_hist)
        lru_state[...] = jnp.zeros_like(lru_state)
        s5_state_r[...] = jnp.zeros_like(s5_state_r)
        s5_state_i[...] = jnp.zeros_like(s5_state_i)

    h = _to_time_major(h_ref, hslab, MIX_TILE_STEPS) if batch_major_in else h_ref[...]
    xn = _rmsnorm(h, gain_ref[...]).astype(_BF16)

    lru_x = _dot(xn, w_in_ref[:, 0:LRU_WIDTH])
    u_s[...] = _dot(xn, w_in_ref[:, 2 * LRU_WIDTH:PROJ_A_WIDTH])

    xc = _causal_conv(lru_x, conv_hist[...], convw_ref, convb_ref[...], slice(None))
    conv_hist[...] = lru_x[tm - (LRU_CONV - 1) * BATCH:]
    xcb = xc.astype(_BF16)
    halves = range(LRU_WIDTH // MXU_WIDTH)
    pre_r = jnp.concatenate([_dot(xcb[:, MXU_WIDTH * p:MXU_WIDTH * (p + 1)], wa_ref[p]) for p in halves], axis=-1)
    pre_i = jnp.concatenate([_dot(xcb[:, MXU_WIDTH * p:MXU_WIDTH * (p + 1)], wx_ref[p]) for p in halves], axis=-1)

    ub = u_s[...].astype(_BF16)
    for j in range(S5_STATES // MXU_WIDTH):
        slab = ub[:, LANES * (j // 2):LANES * (j // 2 + 1)]
        sr[:, MXU_WIDTH * j:MXU_WIDTH * (j + 1)] = _dot(slab, wbre_ref[j])
        si[:, MXU_WIDTH * j:MXU_WIDTH * (j + 1)] = _dot(slab, wbim_ref[j])

    r = _sigmoid(pre_r + ba_ref[...])
    i = _sigmoid(pre_i + bx_ref[...])
    log_a = (-LRU_C) * r * sp_ref[...]
    a = jnp.exp(log_a)
    la_s[...] = a
    z = -jnp.tanh(log_a) * (1.0 + a * a)
    lb_s[...] = jnp.where(z == 0.0, 0.0, z * lax.rsqrt(z)) * (i * xc)

    for sg in range(S5_STATES // S5_SCAN_LANES):
        sl = slice(sg * S5_SCAN_LANES, (sg + 1) * S5_SCAN_LANES)
        ar = jnp.broadcast_to(are_ref[:, sl], (BATCH, S5_SCAN_LANES))
        ai = jnp.broadcast_to(aim_ref[:, sl], (BATCH, S5_SCAN_LANES))
        xr = s5_state_r[:, sl]
        xi = s5_state_i[:, sl]
        for t in range(MIX_TILE_STEPS):
            rows = slice(t * BATCH, (t + 1) * BATCH)
            xr, xi = (ar * xr - ai * xi + sr[rows, sl], ar * xi + ai * xr + si[rows, sl])
            sr[rows, sl] = xr
            si[rows, sl] = xi
        s5_state_r[:, sl] = xr
        s5_state_i[:, sl] = xi

    def project_qkvg(j):
        res = _dot(xn, w_in_ref[:, PROJ_A_WIDTH + j * 512:PROJ_A_WIDTH + (j + 1) * 512])
        for s in range(512 // LANES):
            qslab[j * (512 // LANES) + s] = res[:, s * LANES:(s + 1) * LANES]

    gate_s[...] = _dot(xn, w_in_ref[:, LRU_WIDTH:2 * LRU_WIDTH])
    project_qkvg(0)
    project_qkvg(1)

    hs = lru_state[...]
    for t in range(MIX_TILE_STEPS):
        rows = slice(t * BATCH, (t + 1) * BATCH)
        hs = la_s[rows, :] * hs + lb_s[rows, :]
        lb_s[rows, :] = hs
    lru_state[...] = hs

    ys = []
    for s in range(S5_WIDTH // LANES):
        ksl = slice(512 * s, 512 * (s + 1))
        ys.append(_dot(sr[:, ksl].astype(_BF16), cre_ref[s])
                  - _dot(si[:, ksl].astype(_BF16), cim_ref[s]))
    y = jnp.concatenate(ys, axis=-1) + d_ref[...] * u_s[...]
    z = _gelu(y)
    glu = _dot(z.astype(_BF16), wglu_ref[...])

    y_lru = lb_s[...] * _gelu(gate_s[...])
    ya_ref[:, 0:LRU_WIDTH] = _rmsnorm(y_lru, lrun_ref[...]).astype(ya_ref.dtype)
    out = z * _sigmoid(glu + bglu_ref[...])
    ya_ref[:, LRU_WIDTH:MIX_A_WIDTH] = _rmsnorm(out, s5n_ref[...]).astype(ya_ref.dtype)
    project_qkvg(2)
    project_qkvg(3)
    _from_time_major(qslab, qkvg_ref, MIX_TILE_STEPS)


def _layer_spec(arr, l):
    nd = arr.ndim
    return pl.BlockSpec((None,) + arr.shape[1:], lambda *_: (l,) + (0,) * (nd - 1),
                        pipeline_mode=pl.Buffered(1))


def _stream_spec(batch_major, steps, width):
    if batch_major:
        return pl.BlockSpec((BATCH, steps, width), lambda i: (0, i, 0))
    return pl.BlockSpec((steps * BATCH, width), lambda i: (i, 0))


def _mix_call(h, consts, l, batch_major_in):
    tm = MIX_TILE_ROWS
    return pl.pallas_call(
        functools.partial(_mix_kernel, batch_major_in=batch_major_in),
        grid=(ROWS // tm,),
        in_specs=[_stream_spec(batch_major_in, MIX_TILE_STEPS, D_MODEL)] + [_layer_spec(c, l) for c in consts],
        out_specs=[pl.BlockSpec((tm, MIX_A_WIDTH), lambda i: (i, 0)),
                   _stream_spec(True, MIX_TILE_STEPS, QKVG_WIDTH)],
        out_shape=[jax.ShapeDtypeStruct((ROWS, MIX_A_WIDTH), _BF16),
                   jax.ShapeDtypeStruct((BATCH, SEQ, QKVG_WIDTH), _BF16)],
        scratch_shapes=[pltpu.VMEM(((LRU_CONV - 1) * BATCH, LRU_WIDTH), _F32),
                        pltpu.VMEM((tm, LRU_WIDTH), _F32),
                        pltpu.VMEM((tm, S5_WIDTH), _F32),
                        pltpu.VMEM((tm, LRU_WIDTH), _F32),
                        pltpu.VMEM((tm, LRU_WIDTH), _F32),
                        pltpu.VMEM((BATCH, LRU_WIDTH), _F32),
                        pltpu.VMEM((tm, S5_STATES), _F32),
                        pltpu.VMEM((tm, S5_STATES), _F32),
                        pltpu.VMEM((BATCH, S5_STATES), _F32),
                        pltpu.VMEM((BATCH, S5_STATES), _F32),
                        pltpu.VMEM((QKVG_WIDTH // LANES, tm, LANES), _F32),
                        pltpu.VMEM((D_MODEL // LANES, tm, LANES), _F32)],
        compiler_params=pltpu.CompilerParams(dimension_semantics=("arbitrary",),
                                             vmem_limit_bytes=VMEM_LIMIT_BYTES),
        name="mix_lru_s5",
    )(h, *consts)


def _ret_kernel(qkvg_ref, cos_ref, sin_ref, gn_ref, o_ref,
                state_ref, decay_ref, qdec_ref, kdec_ref):
    C = RET_CHUNK
    Dh = RET_HEAD_DIM
    log_gammas = [math.log1p(-(2.0 ** (-5.0 - h))) for h in range(RET_HEADS)]

    @pl.when((pl.program_id(0) == 0) & (pl.program_id(1) == 0))
    def _tables():
        row = lax.broadcasted_iota(jnp.int32, (C, C), 0)
        col = lax.broadcasted_iota(jnp.int32, (C, C), 1)
        rel = (row - col).astype(_F32)
        idx = lax.broadcasted_iota(jnp.int32, (C, Dh), 0).astype(_F32)
        for h in range(RET_HEADS):
            lg = log_gammas[h]
            decay_ref[h] = jnp.where(rel >= 0.0, jnp.exp(lg * jnp.maximum(rel, 0.0)), 0.0) * (Dh ** -0.5)
            qdec_ref[h] = jnp.exp(lg * (idx + 1.0))
            kdec_ref[h] = jnp.exp(lg * (C - 1.0 - idx)) * (Dh ** -0.5)

    @pl.when(pl.program_id(1) == 0)
    def _init():
        state_ref[...] = jnp.zeros_like(state_ref)

    cos2 = cos_ref[0]
    sin2 = sin_ref[0]

    def rot(t):
        return t * cos2 + pltpu.roll(t, Dh // 2, 1) * sin2

    for h in range(RET_HEADS):
        hs = slice(h * Dh, (h + 1) * Dh)
        qh = rot(qkvg_ref[:, h * Dh:(h + 1) * Dh].astype(_F32))
        kh = rot(qkvg_ref[:, RET_WIDTH + h * Dh:RET_WIDTH + (h + 1) * Dh].astype(_F32))
        vb = qkvg_ref[:, 2 * RET_WIDTH + h * Dh:2 * RET_WIDTH + (h + 1) * Dh]
        scores = lax.dot_general(qh.astype(_BF16), kh.astype(_BF16), (((1,), (1,)), ((), ())),
                                 preferred_element_type=_F32) * decay_ref[h]
        intra = _dot(scores.astype(_BF16), vb)
        state = state_ref[h]
        cross = _dot((qh * qdec_ref[h]).astype(_BF16), state.astype(_BF16))
        kv = lax.dot_general((kh * kdec_ref[h]).astype(_BF16), vb, (((0,), (0,)), ((), ())),
                             preferred_element_type=_F32)
        state_ref[h] = math.exp(log_gammas[h] * C) * state + kv
        o = intra + cross
        mu = jnp.mean(o, axis=-1, keepdims=True)
        oc = o - mu
        var = jnp.mean(oc * oc, axis=-1, keepdims=True)
        on = oc * lax.rsqrt(var + NORM_EPS) * gn_ref[:, hs]
        g = qkvg_ref[:, 3 * RET_WIDTH + h * Dh:3 * RET_WIDTH + (h + 1) * Dh].astype(_F32)
        o_ref[:, hs] = (on * (g * _sigmoid(g))).astype(o_ref.dtype)


def _ret_call(qkvg, cos2, sin2, gn_gain, l):
    C = RET_CHUNK
    return pl.pallas_call(
        _ret_kernel,
        grid=(BATCH, SEQ // C),
        in_specs=[pl.BlockSpec((None, C, QKVG_WIDTH), lambda b, n: (b, n, 0)),
                  pl.BlockSpec((1, C, RET_HEAD_DIM), lambda b, n: (b, n, 0)),
                  pl.BlockSpec((1, C, RET_HEAD_DIM), lambda b, n: (b, n, 0)),
                  pl.BlockSpec((None, 1, RET_WIDTH), lambda b, n: (l, 0, 0))],
        out_specs=pl.BlockSpec((None, C, RET_WIDTH), lambda b, n: (b, n, 0)),
        out_shape=jax.ShapeDtypeStruct((BATCH, SEQ, RET_WIDTH), _F32),
        scratch_shapes=[pltpu.VMEM((RET_HEADS, RET_HEAD_DIM, RET_HEAD_DIM), _F32),
                        pltpu.VMEM((RET_HEADS, C, C), _F32),
                        pltpu.VMEM((RET_HEADS, C, RET_HEAD_DIM), _F32),
                        pltpu.VMEM((RET_HEADS, C, RET_HEAD_DIM), _F32)],
        compiler_params=pltpu.CompilerParams(dimension_semantics=("arbitrary", "arbitrary"),
                                             vmem_limit_bytes=VMEM_LIMIT_BYTES),
        name="retention",
    )(qkvg, cos2, sin2, gn_gain)


def _ffn_kernel(h_ref, ya_ref, yr_ref, wo_ref, gain_ref, w_up_ref, convw_ref, convb_ref,
                w_down_ref, fin_ref, o_ref, acc, carry, yslab, hslab, *, batch_major_in, last):
    tm = FFN_TILE_ROWS
    fc = FFN_CHUNK

    @pl.when(pl.program_id(0) == 0)
    def _init():
        carry[...] = jnp.zeros_like(carry)

    h = _to_time_major(h_ref, hslab, FFN_TILE_STEPS) if batch_major_in else h_ref[...]
    yr = _to_time_major(yr_ref, yslab, FFN_TILE_STEPS).astype(_BF16)
    h1 = _dot(ya_ref[...], wo_ref[0:MIX_A_WIDTH, :]) + _dot(yr, wo_ref[MIX_A_WIDTH:, :]) + h
    acc[...] = h1
    xn = _rmsnorm(h1, gain_ref[...]).astype(_BF16)

    def up_proj(j):
        return (_dot(xn, w_up_ref[:, j * fc:(j + 1) * fc]),
                _dot(xn, w_up_ref[:, D_FF + j * fc:D_FF + (j + 1) * fc]))

    n_chunks = D_FF // fc
    ups = up_proj(0)
    for j in range(n_chunks):
        nxt = up_proj(j + 1) if j + 1 < n_chunks else None
        vs = slice(j * fc, (j + 1) * fc)
        gs = slice(D_FF + j * fc, D_FF + (j + 1) * fc)
        upv, upg = ups
        cv = _causal_conv(upv, carry[:, vs], convw_ref, convb_ref[:, vs], vs)
        cg = _causal_conv(upg, carry[:, gs], convw_ref, convb_ref[:, gs], gs)
        carry[:, vs] = upv[tm - (FFN_CONV - 1) * BATCH:]
        carry[:, gs] = upg[tm - (FFN_CONV - 1) * BATCH:]
        act = (_gelu(cg) * cv).astype(_BF16)
        acc[...] += _dot(act, w_down_ref[vs, :])
        ups = nxt
    if last:
        out = _rmsnorm(acc[...], fin_ref[...])
        for s in range(D_MODEL // LANES):
            hslab[s] = out[:, s * LANES:(s + 1) * LANES]
        _from_time_major(hslab, o_ref, FFN_TILE_STEPS)
    else:
        o_ref[...] = acc[...]


def _ffn_call(h, ya, yr, consts, fin_gain, l, batch_major_in, last):
    tm = FFN_TILE_ROWS
    out_shape = (BATCH, SEQ, D_MODEL) if last else (ROWS, D_MODEL)
    return pl.pallas_call(
        functools.partial(_ffn_kernel, batch_major_in=batch_major_in, last=last),
        grid=(ROWS // tm,),
        in_specs=[_stream_spec(batch_major_in, FFN_TILE_STEPS, D_MODEL),
                  pl.BlockSpec((tm, MIX_A_WIDTH), lambda i: (i, 0)),
                  _stream_spec(True, FFN_TILE_STEPS, RET_WIDTH)]
                 + [_layer_spec(c, l) for c in consts] + [_layer_spec(fin_gain, 0)],
        out_specs=_stream_spec(last, FFN_TILE_STEPS, D_MODEL),
        out_shape=jax.ShapeDtypeStruct(out_shape, _F32),
        scratch_shapes=[pltpu.VMEM((tm, D_MODEL), _F32),
                        pltpu.VMEM(((FFN_CONV - 1) * BATCH, 2 * D_FF), _F32),
                        pltpu.VMEM((RET_WIDTH // LANES, tm, LANES), _F32),
                        pltpu.VMEM((D_MODEL // LANES, tm, LANES), _F32)],
        compiler_params=pltpu.CompilerParams(dimension_semantics=("arbitrary",),
                                             vmem_limit_bytes=VMEM_LIMIT_BYTES),
        name="outproj_ffn",
    )(h, ya, yr, *consts, fin_gain)


def _pack_gate(w):
    blk = LRU_WIDTH // LRU_BLOCKS
    per = MXU_WIDTH // blk
    w5 = w.astype(_F32).reshape(DEPTH, LRU_BLOCKS // per, per, blk, blk)
    t = jnp.einsum('lphij,hg->lphigj', w5, jnp.eye(per, dtype=_F32))
    return t.reshape(DEPTH, LRU_BLOCKS // per, MXU_WIDTH, MXU_WIDTH).astype(_BF16)


def _s5_params(lam_re, lam_im, log_dt, b_re, b_im, c_re, c_im):
    dt = jnp.exp(log_dt.astype(_F32))[..., None]
    lr, li = lam_re.astype(_F32), lam_im.astype(_F32)
    mag = jnp.exp(lr * dt)
    abar_re, abar_im = mag * jnp.cos(li * dt), mag * jnp.sin(li * dt)
    den = lr * lr + li * li
    nr, ni = abar_re - 1.0, abar_im
    coef_re = ((nr * lr + ni * li) / den)[..., None]
    coef_im = ((ni * lr - nr * li) / den)[..., None]
    br, bi = b_re.astype(_F32), b_im.astype(_F32)
    bbar_re = coef_re * br - coef_im * bi
    bbar_im = coef_re * bi + coef_im * br

    n_bt = S5_STATES // MXU_WIDTH
    g_out = MXU_WIDTH // S5_STATE
    g_in = LANES // S5_GROUP
    sel = (jnp.arange(g_in)[None, :, None]
           == (g_out * (jnp.arange(n_bt) % 2))[:, None, None] + jnp.arange(g_out)[None, None, :]).astype(_F32)

    def pack_b(bbar):
        bb = bbar.reshape(DEPTH, n_bt, g_out, S5_STATE, S5_GROUP)
        t = jnp.einsum('ljopc,jio->ljicop', bb, sel)
        return t.reshape(DEPTH, n_bt, LANES, MXU_WIDTH).astype(_BF16)

    def pack_c(c):
        n_sl = S5_WIDTH // LANES
        cc = c.astype(_F32).reshape(DEPTH, n_sl, g_in, S5_GROUP, S5_STATE)
        t = jnp.einsum('lsicp,io->lsipoc', cc, jnp.eye(g_in, dtype=_F32))
        return t.reshape(DEPTH, n_sl, g_in * S5_STATE, LANES).astype(_BF16)

    return (abar_re.reshape(DEPTH, 1, S5_STATES), abar_im.reshape(DEPTH, 1, S5_STATES),
            pack_b(bbar_re), pack_b(bbar_im), pack_c(c_re), pack_c(c_im))


def kernel(x, positions, norm_mix, w_in, lru_conv_w, lru_conv_b, lru_wa, lru_ba, lru_wx, lru_bx, lru_lambda, lru_norm, s5_lambda_re, s5_lambda_im, s5_log_dt, s5_b_re, s5_b_im, s5_c_re, s5_c_im, s5_d, s5_w_glu, s5_b_glu, s5_norm, ret_norm, w_out, norm_ffn, w_up, ffn_conv_w, ffn_conv_b, w_down, norm_final):
    cos2, sin2 = _rotary_tables(positions)
    row = lambda v: v.reshape(v.shape[0], 1, -1).astype(_F32)
    mix_consts = [row(norm_mix), w_in.astype(_BF16),
                  lru_conv_w.astype(_F32), row(lru_conv_b),
                  _pack_gate(lru_wa), row(lru_ba), _pack_gate(lru_wx), row(lru_bx),
                  row(jax.nn.softplus(-lru_lambda.astype(_F32))), row(lru_norm),
                  *_s5_params(s5_lambda_re, s5_lambda_im, s5_log_dt, s5_b_re, s5_b_im, s5_c_re, s5_c_im),
                  row(s5_d), s5_w_glu.astype(_BF16), row(s5_b_glu), row(s5_norm)]
    ffn_consts = [w_out.astype(_BF16), row(norm_ffn), w_up.astype(_BF16),
                  ffn_conv_w.astype(_F32), row(ffn_conv_b), w_down.astype(_BF16)]
    gn_gain = row(ret_norm)
    fin_gain = norm_final.reshape(1, 1, D_MODEL).astype(_F32)
    h = x
    for l in range(DEPTH):
        first, last = l == 0, l == DEPTH - 1
        ya, qkvg = _mix_call(h, mix_consts, l, batch_major_in=first)
        yr = _ret_call(qkvg, cos2, sin2, gn_gain, l)
        h = _ffn_call(h, ya, yr, ffn_consts, fin_gain, l, batch_major_in=first, last=last)
    return h
```

```python
import functools
import math

import jax
import jax.numpy as jnp
from jax import lax
from jax.experimental import pallas as pl
from jax.experimental.pallas import tpu as pltpu

D_MODEL = 1024
BATCH = 8
SEQ = 2048
DEPTH = 2
ROWS = BATCH * SEQ

LRU_WIDTH = 512
LRU_BLOCKS = 8
LRU_CONV = 4
LRU_C = 8.0
S5_WIDTH = 512
S5_GROUP = 16
S5_GROUPS = 32
S5_STATE = 64
S5_STATES = S5_GROUPS * S5_STATE
RET_HEADS = 4
RET_HEAD_DIM = 128
RET_WIDTH = 512
ROPE_BASE = 10000.0
MIX_A_WIDTH = LRU_WIDTH + S5_WIDTH
PROJ_A_WIDTH = 2 * LRU_WIDTH + S5_WIDTH
QKVG_WIDTH = 4 * RET_WIDTH
IN_WIDTH = PROJ_A_WIDTH + QKVG_WIDTH
D_FF = 3 * D_MODEL
FFN_CONV = 3
NORM_EPS = 1e-6

SUBLANES = 8
LANES = 128
MXU_WIDTH = 256
VMEM_LIMIT_BYTES = 56 * 1024 * 1024

MIX_TILE_STEPS = 64
MIX_TILE_ROWS = MIX_TILE_STEPS * BATCH
FFN_TILE_STEPS = 64
FFN_TILE_ROWS = FFN_TILE_STEPS * BATCH
FFN_CHUNK = 1024
RET_CHUNK = 256
S5_SCAN_LANES = 512

_F32 = jnp.float32
_BF16 = jnp.bfloat16


def _gelu(x):
    return 0.5 * x * (1.0 + jnp.tanh(0.7978845608028654 * (x + 0.044715 * (x * x * x))))


def _sigmoid(x):
    return 1.0 / (1.0 + jnp.exp(-x))


def _rmsnorm(x, gain):
    return x * lax.rsqrt(jnp.mean(x * x, axis=-1, keepdims=True) + NORM_EPS) * gain


def _dot(a, b):
    return jnp.dot(a, b, preferred_element_type=_F32)


def _to_time_major(x_ref, slab, steps):
    n = x_ref.shape[-1] // LANES
    for b in range(BATCH):
        for s in range(n):
            slab[s, pl.ds(b, steps, stride=BATCH), :] = x_ref[b, :, s * LANES:(s + 1) * LANES]
    return jnp.concatenate([slab[s] for s in range(n)], axis=-1)


def _from_time_major(slab, o_ref, steps):
    n = o_ref.shape[-1] // LANES
    for b in range(BATCH):
        for s in range(n):
            o_ref[b, :, s * LANES:(s + 1) * LANES] = (
                slab[s, pl.ds(b, steps, stride=BATCH), :].astype(o_ref.dtype))


def _causal_conv(x, hist, w_ref, bias, cols):
    taps = w_ref.shape[0]
    tm = x.shape[0]
    y = bias
    for k in range(taps):
        back = (taps - 1 - k) * BATCH
        xs = x if back == 0 else jnp.concatenate([hist[hist.shape[0] - back:], x[:tm - back]], axis=0)
        y = y + w_ref[k:k + 1, cols] * xs
    return y


def _rotary_kernel(pos_ref, inv_ref, sign_ref, cos_ref, sin_ref):
    half = RET_HEAD_DIM // 2
    pairs = SEQ // 2
    low = lax.broadcasted_iota(jnp.int32, (pairs, RET_HEAD_DIM), 1) < half
    pos = jnp.where(low, pos_ref[0, :, 0:1], pos_ref[0, :, 1:2]).astype(_F32)
    ang = pos * inv_ref[...]
    for table, out_ref, scale in ((jnp.cos(ang), cos_ref, None), (jnp.sin(ang), sin_ref, sign_ref[...])):
        swapped = pltpu.roll(table, half, 1)
        even = jnp.where(low, table, swapped)
        odd = jnp.where(low, swapped, table)
        if scale is not None:
            even, odd = even * scale, odd * scale
        out_ref[0, pl.ds(0, pairs, stride=2), :] = even
        out_ref[0, pl.ds(1, pairs, stride=2), :] = odd


def _rotary_tables(positions):
    half = RET_HEAD_DIM // 2
    inv = ROPE_BASE ** (-jnp.arange(half, dtype=_F32) * 2.0 / RET_HEAD_DIM)
    inv2 = jnp.concatenate([inv, inv])[None, :]
    sign = jnp.concatenate([-jnp.ones((half,), _F32), jnp.ones((half,), _F32)])[None, :]
    pos3 = positions.reshape(BATCH, SEQ // 2, 2)
    out = jax.ShapeDtypeStruct((BATCH, SEQ, RET_HEAD_DIM), _F32)
    return pl.pallas_call(
        _rotary_kernel,
        grid=(BATCH,),
        in_specs=[pl.BlockSpec((1, SEQ // 2, 2), lambda b: (b, 0, 0)),
                  pl.BlockSpec((1, RET_HEAD_DIM), lambda b: (0, 0)),
                  pl.BlockSpec((1, RET_HEAD_DIM), lambda b: (0, 0))],
        out_specs=[pl.BlockSpec((1, SEQ, RET_HEAD_DIM), lambda b: (b, 0, 0)),
                   pl.BlockSpec((1, SEQ, RET_HEAD_DIM), lambda b: (b, 0, 0))],
        out_shape=[out, out],
        name="rotary_tables",
    )(pos3, inv2, sign)


def _mix_kernel(h_ref, gain_ref, w_in_ref,
                convw_ref, convb_ref, wa_ref, ba_ref, wx_ref, bx_ref, sp_ref, lrun_ref,
                are_ref, aim_ref, wbre_ref, wbim_ref, cre_ref, cim_ref, d_ref,
                wglu_ref, bglu_ref, s5n_ref,
                ya_ref, qkvg_ref,
                conv_hist, gate_s, u_s, la_s, lb_s, lru_state, sr, si, s5_state_r, s5_state_i, qslab, hslab,
                *, batch_major_in):
    tm = MIX_TILE_ROWS

    @pl.when(pl.program_id(0) == 0)
    def _init():
        conv_hist[...] = jnp.zeros_like(conv_hist)
        lru_state[...] = jnp.zeros_like(lru_state)
        s5_state_r[...] = jnp.zeros_like(s5_state_r)
        s5_state_i[...] = jnp.zeros_like(s5_state_i)

    h = _to_time_major(h_ref, hslab, MIX_TILE_STEPS) if batch_major_in else h_ref[...]
    xn = _rmsnorm(h, gain_ref[...]).astype(_BF16)

    lru_x = _dot(xn, w_in_ref[:, 0:LRU_WIDTH])
    u_s[...] = _dot(xn, w_in_ref[:, 2 * LRU_WIDTH:PROJ_A_WIDTH])

    xc = _causal_conv(lru_x, conv_hist[...], convw_ref, convb_ref[...], slice(None))
    conv_hist[...] = lru_x[tm - (LRU_CONV - 1) * BATCH:]
    xcb = xc.astype(_BF16)
    halves = range(LRU_WIDTH // MXU_WIDTH)
    pre_r = jnp.concatenate([_dot(xcb[:, MXU_WIDTH * p:MXU_WIDTH * (p + 1)], wa_ref[p]) for p in halves], axis=-1)
    pre_i = jnp.concatenate([_dot(xcb[:, MXU_WIDTH * p:MXU_WIDTH * (p + 1)], wx_ref[p]) for p in halves], axis=-1)

    ub = u_s[...].astype(_BF16)
    for j in range(S5_STATES // MXU_WIDTH):
        slab = ub[:, LANES * (j // 2):LANES * (j // 2 + 1)]
        sr[:, MXU_WIDTH * j:MXU_WIDTH * (j + 1)] = _dot(slab, wbre_ref[j])
        si[:, MXU_WIDTH * j:MXU_WIDTH * (j + 1)] = _dot(slab, wbim_ref[j])

    r = _sigmoid(pre_r + ba_ref[...])
    i = _sigmoid(pre_i + bx_ref[...])
    log_a = (-LRU_C) * r * sp_ref[...]
    a = jnp.exp(log_a)
    la_s[...] = a
    z = -jnp.tanh(log_a) * (1.0 + a * a)
    lb_s[...] = jnp.where(z == 0.0, 0.0, z * lax.rsqrt(z)) * (i * xc)

    for sg in range(S5_STATES // S5_SCAN_LANES):
        sl = slice(sg * S5_SCAN_LANES, (sg + 1) * S5_SCAN_LANES)
        ar = jnp.broadcast_to(are_ref[:, sl], (BATCH, S5_SCAN_LANES))
        ai = jnp.broadcast_to(aim_ref[:, sl], (BATCH, S5_SCAN_LANES))
        xr = s5_state_r[:, sl]
        xi = s5_state_i[:, sl]
        for t in range(MIX_TILE_STEPS):
            rows = slice(t * BATCH, (t + 1) * BATCH)
            xr, xi = (ar * xr - ai * xi + sr[rows, sl], ar * xi + ai * xr + si[rows, sl])
            sr[rows, sl] = xr
            si[rows, sl] = xi
        s5_state_r[:, sl] = xr
        s5_state_i[:, sl] = xi

    def project_qkvg(j):
        res = _dot(xn, w_in_ref[:, PROJ_A_WIDTH + j * 512:PROJ_A_WIDTH + (j + 1) * 512])
        for s in range(512 // LANES):
            qslab[j * (512 // LANES) + s] = res[:, s * LANES:(s + 1) * LANES]

    gate_s[...] = _dot(xn, w_in_ref[:, LRU_WIDTH:2 * LRU_WIDTH])
    project_qkvg(0)
    project_qkvg(1)

    hs = lru_state[...]
    for t in range(MIX_TILE_STEPS):
        rows = slice(t * BATCH, (t + 1) * BATCH)
        hs = la_s[rows, :] * hs + lb_s[rows, :]
        lb_s[rows, :] = hs
    lru_state[...] = hs

    ys = []
    for s in range(S5_WIDTH // LANES):
        ksl = slice(512 * s, 512 * (s + 1))
        ys.append(_dot(sr[:, ksl].astype(_BF16), cre_ref[s])
                  - _dot(si[:, ksl].astype(_BF16), cim_ref[s]))
    y = jnp.concatenate(ys, axis=-1) + d_ref[...] * u_s[...]
    z = _gelu(y)
    glu = _dot(z.astype(_BF16), wglu_ref[...])

    y_lru = lb_s[...] * _gelu(gate_s[...])
    ya_ref[:, 0:LRU_WIDTH] = _rmsnorm(y_lru, lrun_ref[...]).astype(ya_ref.dtype)
    out = z * _sigmoid(glu + bglu_ref[...])
    ya_ref[:, LRU_WIDTH:MIX_A_WIDTH] = _rmsnorm(out, s5n_ref[...]).astype(ya_ref.dtype)
    project_qkvg(2)
    project_qkvg(3)
    _from_time_major(qslab, qkvg_ref, MIX_TILE_STEPS)


def _layer_spec(arr, l):
    nd = arr.ndim
    return pl.BlockSpec((None,) + arr.shape[1:], lambda *_: (l,) + (0,) * (nd - 1),
                        pipeline_mode=pl.Buffered(1))


def _stream_spec(batch_major, steps, width):
    if batch_major:
        return pl.BlockSpec((BATCH, steps, width), lambda i: (0, i, 0))
    return pl.BlockSpec((steps * BATCH, width), lambda i: (i, 0))


def _mix_call(h, consts, l, batch_major_in):
    tm = MIX_TILE_ROWS
    return pl.pallas_call(
        functools.partial(_mix_kernel, batch_major_in=batch_major_in),
        grid=(ROWS // tm,),
        in_specs=[_stream_spec(batch_major_in, MIX_TILE_STEPS, D_MODEL)] + [_layer_spec(c, l) for c in consts],
        out_specs=[pl.BlockSpec((tm, MIX_A_WIDTH), lambda i: (i, 0)),
                   _stream_spec(True, MIX_TILE_STEPS, QKVG_WIDTH)],
        out_shape=[jax.ShapeDtypeStruct((ROWS, MIX_A_WIDTH), _BF16),
                   jax.ShapeDtypeStruct((BATCH, SEQ, QKVG_WIDTH), _BF16)],
        scratch_shapes=[pltpu.VMEM(((LRU_CONV - 1) * BATCH, LRU_WIDTH), _F32),
                        pltpu.VMEM((tm, LRU_WIDTH), _F32),
                        pltpu.VMEM((tm, S5_WIDTH), _F32),
                        pltpu.VMEM((tm, LRU_WIDTH), _F32),
                        pltpu.VMEM((tm, LRU_WIDTH), _F32),
                        pltpu.VMEM((BATCH, LRU_WIDTH), _F32),
                        pltpu.VMEM((tm, S5_STATES), _F32),
                        pltpu.VMEM((tm, S5_STATES), _F32),
                        pltpu.VMEM((BATCH, S5_STATES), _F32),
                        pltpu.VMEM((BATCH, S5_STATES), _F32),
                        pltpu.VMEM((QKVG_WIDTH // LANES, tm, LANES), _F32),
                        pltpu.VMEM((D_MODEL // LANES, tm, LANES), _F32)],
        compiler_params=pltpu.CompilerParams(dimension_semantics=("arbitrary",),
                                             vmem_limit_bytes=VMEM_LIMIT_BYTES),
        name="mix_lru_s5",
    )(h, *consts)


def _ret_kernel(qkvg_ref, cos_ref, sin_ref, gn_ref, o_ref,
                state_ref, decay_ref, qdec_ref, kdec_ref):
    C = RET_CHUNK
    Dh = RET_HEAD_DIM
    log_gammas = [math.log1p(-(2.0 ** (-5.0 - h))) for h in range(RET_HEADS)]

    @pl.when((pl.program_id(0) == 0) & (pl.program_id(1) == 0))
    def _tables():
        row = lax.broadcasted_iota(jnp.int32, (C, C), 0)
        col = lax.broadcasted_iota(jnp.int32, (C, C), 1)
        rel = (row - col).astype(_F32)
        idx = lax.broadcasted_iota(jnp.int32, (C, Dh), 0).astype(_F32)
        for h in range(RET_HEADS):
            lg = log_gammas[h]
            decay_ref[h] = jnp.where(rel >= 0.0, jnp.exp(lg * jnp.maximum(rel, 0.0)), 0.0) * (Dh ** -0.5)
            qdec_ref[h] = jnp.exp(lg * (idx + 1.0))
            kdec_ref[h] = jnp.exp(lg * (C - 1.0 - idx)) * (Dh ** -0.5)

    @pl.when(pl.program_id(1) == 0)
    def _init():
        state_ref[...] = jnp.zeros_like(state_ref)

    cos2 = cos_ref[0]
    sin2 = sin_ref[0]

    def rot(t):
        return t * cos2 + pltpu.roll(t, Dh // 2, 1) * sin2

    for h in range(RET_HEADS):
        hs = slice(h * Dh, (h + 1) * Dh)
        qh = rot(qkvg_ref[:, h * Dh:(h + 1) * Dh].astype(_F32))
        kh = rot(qkvg_ref[:, RET_WIDTH + h * Dh:RET_WIDTH + (h + 1) * Dh].astype(_F32))
        vb = qkvg_ref[:, 2 * RET_WIDTH + h * Dh:2 * RET_WIDTH + (h + 1) * Dh]
        scores = lax.dot_general(qh.astype(_BF16), kh.astype(_BF16), (((1,), (1,)), ((), ())),
                                 preferred_element_type=_F32) * decay_ref[h]
        intra = _dot(scores.astype(_BF16), vb)
        state = state_ref[h]
        cross = _dot((qh * qdec_ref[h]).astype(_BF16), state.astype(_BF16))
        kv = lax.dot_general((kh * kdec_ref[h]).astype(_BF16), vb, (((0,), (0,)), ((), ())),
                             preferred_element_type=_F32)
        state_ref[h] = math.exp(log_gammas[h] * C) * state + kv
        o = intra + cross
        mu = jnp.mean(o, axis=-1, keepdims=True)
        oc = o - mu
        var = jnp.mean(oc * oc, axis=-1, keepdims=True)
        on = oc * lax.rsqrt(var + NORM_EPS) * gn_ref[:, hs]
        g = qkvg_ref[:, 3 * RET_WIDTH + h * Dh:3 * RET_WIDTH + (h + 1) * Dh].astype(_F32)
        o_ref[:, hs] = (on * (g * _sigmoid(g))).astype(o_ref.dtype)


def _ret_call(qkvg, cos2, sin2, gn_gain, l):
    C = RET_CHUNK
    return pl.pallas_call(
        _ret_kernel,
        grid=(BATCH, SEQ // C),
        in_specs=[pl.BlockSpec((None, C, QKVG_WIDTH), lambda b, n: (b, n, 0)),
                  pl.BlockSpec((1, C, RET_HEAD_DIM), lambda b, n: (b, n, 0)),
                  pl.BlockSpec((1, C, RET_HEAD_DIM), lambda b, n: (b, n, 0)),
                  pl.BlockSpec((None, 1, RET_WIDTH), lambda b, n: (l, 0, 0))],
        out_specs=pl.BlockSpec((None, C, RET_WIDTH), lambda b, n: (b, n, 0)),
        out_shape=jax.ShapeDtypeStruct((BATCH, SEQ, RET_WIDTH), _F32),
        scratch_shapes=[pltpu.VMEM((RET_HEADS, RET_HEAD_DIM, RET_HEAD_DIM), _F32),
                        pltpu.VMEM((RET_HEADS, C, C), _F32),
                        pltpu.VMEM((RET_HEADS, C, RET_HEAD_DIM), _F32),
                        pltpu.VMEM((RET_HEADS, C, RET_HEAD_DIM), _F32)],
        compiler_params=pltpu.CompilerParams(dimension_semantics=("arbitrary", "arbitrary"),
                                             vmem_limit_bytes=VMEM_LIMIT_BYTES),
        name="retention",
    )(qkvg, cos2, sin2, gn_gain)


def _ffn_kernel(h_ref, ya_ref, yr_ref, wo_ref, gain_ref, w_up_ref, convw_ref, convb_ref,
                w_down_ref, fin_ref, o_ref, acc, carry, yslab, hslab, *, batch_major_in, last):
    tm = FFN_TILE_ROWS
    fc = FFN_CHUNK

    @pl.when(pl.program_id(0) == 0)
    def _init():
        carry[...] = jnp.zeros_like(carry)

    h = _to_time_major(h_ref, hslab, FFN_TILE_STEPS) if batch_major_in else h_ref[...]
    yr = _to_time_major(yr_ref, yslab, FFN_TILE_STEPS).astype(_BF16)
    h1 = _dot(ya_ref[...], wo_ref[0:MIX_A_WIDTH, :]) + _dot(yr, wo_ref[MIX_A_WIDTH:, :]) + h
    acc[...] = h1
    xn = _rmsnorm(h1, gain_ref[...]).astype(_BF16)

    def up_proj(j):
        return (_dot(xn, w_up_ref[:, j * fc:(j + 1) * fc]),
                _dot(xn, w_up_ref[:, D_FF + j * fc:D_FF + (j + 1) * fc]))

    n_chunks = D_FF // fc
    ups = up_proj(0)
    for j in range(n_chunks):
        nxt = up_proj(j + 1) if j + 1 < n_chunks else None
        vs = slice(j * fc, (j + 1) * fc)
        gs = slice(D_FF + j * fc, D_FF + (j + 1) * fc)
        upv, upg = ups
        cv = _causal_conv(upv, carry[:, vs], convw_ref, convb_ref[:, vs], vs)
        cg = _causal_conv(upg, carry[:, gs], convw_ref, convb_ref[:, gs], gs)
        carry[:, vs] = upv[tm - (FFN_CONV - 1) * BATCH:]
        carry[:, gs] = upg[tm - (FFN_CONV - 1) * BATCH:]
        act = (_gelu(cg) * cv).astype(_BF16)
        down = _dot(act, w_down_ref[vs, :])
        if j + 1 < n_chunks:
            acc[...] += down
        ups = nxt
    total = acc[...] + down
    if last:
        out = _rmsnorm(total, fin_ref[...])
        for s in range(D_MODEL // LANES):
            hslab[s] = out[:, s * LANES:(s + 1) * LANES]
        _from_time_major(hslab, o_ref, FFN_TILE_STEPS)
    else:
        o_ref[...] = total


def _ffn_call(h, ya, yr, consts, fin_gain, l, batch_major_in, last):
    tm = FFN_TILE_ROWS
    out_shape = (BATCH, SEQ, D_MODEL) if last else (ROWS, D_MODEL)
    return pl.pallas_call(
        functools.partial(_ffn_kernel, batch_major_in=batch_major_in, last=last),
        grid=(ROWS // tm,),
        in_specs=[_stream_spec(batch_major_in, FFN_TILE_STEPS, D_MODEL),
                  pl.BlockSpec((tm, MIX_A_WIDTH), lambda i: (i, 0)),
                  _stream_spec(True, FFN_TILE_STEPS, RET_WIDTH)]
                 + [_layer_spec(c, l) for c in consts] + [_layer_spec(fin_gain, 0)],
        out_specs=_stream_spec(last, FFN_TILE_STEPS, D_MODEL),
        out_shape=jax.ShapeDtypeStruct(out_shape, _F32),
        scratch_shapes=[pltpu.VMEM((tm, D_MODEL), _F32),
                        pltpu.VMEM(((FFN_CONV - 1) * BATCH, 2 * D_FF), _F32),
                        pltpu.VMEM((RET_WIDTH // LANES, tm, LANES), _F32),
                        pltpu.VMEM((D_MODEL // LANES, tm, LANES), _F32)],
        compiler_params=pltpu.CompilerParams(dimension_semantics=("arbitrary",),
                                             vmem_limit_bytes=VMEM_LIMIT_BYTES),
        name="outproj_ffn",
    )(h, ya, yr, *consts, fin_gain)


def _pack_gate(w):
    blk = LRU_WIDTH // LRU_BLOCKS
    per = MXU_WIDTH // blk
    w5 = w.astype(_F32).reshape(DEPTH, LRU_BLOCKS // per, per, blk, blk)
    t = jnp.einsum('lphij,hg->lphigj', w5, jnp.eye(per, dtype=_F32))
    return t.reshape(DEPTH, LRU_BLOCKS // per, MXU_WIDTH, MXU_WIDTH).astype(_BF16)


def _s5_params(lam_re, lam_im, log_dt, b_re, b_im, c_re, c_im):
    dt = jnp.exp(log_dt.astype(_F32))[..., None]
    lr, li = lam_re.astype(_F32), lam_im.astype(_F32)
    mag = jnp.exp(lr * dt)
    abar_re, abar_im = mag * jnp.cos(li * dt), mag * jnp.sin(li * dt)
    den = lr * lr + li * li
    nr, ni = abar_re - 1.0, abar_im
    coef_re = ((nr * lr + ni * li) / den)[..., None]
    coef_im = ((ni * lr - nr * li) / den)[..., None]
    br, bi = b_re.astype(_F32), b_im.astype(_F32)
    bbar_re = coef_re * br - coef_im * bi
    bbar_im = coef_re * bi + coef_im * br

    n_bt = S5_STATES // MXU_WIDTH
    g_out = MXU_WIDTH // S5_STATE
    g_in = LANES // S5_GROUP
    sel = (jnp.arange(g_in)[None, :, None]
           == (g_out * (jnp.arange(n_bt) % 2))[:, None, None] + jnp.arange(g_out)[None, None, :]).astype(_F32)

    def pack_b(bbar):
        bb = bbar.reshape(DEPTH, n_bt, g_out, S5_STATE, S5_GROUP)
        t = jnp.einsum('ljopc,jio->ljicop', bb, sel)
        return t.reshape(DEPTH, n_bt, LANES, MXU_WIDTH).astype(_BF16)

    def pack_c(c):
        n_sl = S5_WIDTH // LANES
        cc = c.astype(_F32).reshape(DEPTH, n_sl, g_in, S5_GROUP, S5_STATE)
        t = jnp.einsum('lsicp,io->lsipoc', cc, jnp.eye(g_in, dtype=_F32))
        return t.reshape(DEPTH, n_sl, g_in * S5_STATE, LANES).astype(_BF16)

    return (abar_re.reshape(DEPTH, 1, S5_STATES), abar_im.reshape(DEPTH, 1, S5_STATES),
            pack_b(bbar_re), pack_b(bbar_im), pack_c(c_re), pack_c(c_im))


def kernel(x, positions, norm_mix, w_in, lru_conv_w, lru_conv_b, lru_wa, lru_ba, lru_wx, lru_bx, lru_lambda, lru_norm, s5_lambda_re, s5_lambda_im, s5_log_dt, s5_b_re, s5_b_im, s5_c_re, s5_c_im, s5_d, s5_w_glu, s5_b_glu, s5_norm, ret_norm, w_out, norm_ffn, w_up, ffn_conv_w, ffn_conv_b, w_down, norm_final):
    cos2, sin2 = _rotary_tables(positions)
    row = lambda v: v.reshape(v.shape[0], 1, -1).astype(_F32)
    mix_consts = [row(norm_mix), w_in.astype(_BF16),
                  lru_conv_w.astype(_F32), row(lru_conv_b),
                  _pack_gate(lru_wa), row(lru_ba), _pack_gate(lru_wx), row(lru_bx),
                  row(jax.nn.softplus(-lru_lambda.astype(_F32))), row(lru_norm),
                  *_s5_params(s5_lambda_re, s5_lambda_im, s5_log_dt, s5_b_re, s5_b_im, s5_c_re, s5_c_im),
                  row(s5_d), s5_w_glu.astype(_BF16), row(s5_b_glu), row(s5_norm)]
    ffn_consts = [w_out.astype(_BF16), row(norm_ffn), w_up.astype(_BF16),
                  ffn_conv_w.astype(_F32), row(ffn_conv_b), w_down.astype(_BF16)]
    gn_gain = row(ret_norm)
    fin_gain = norm_final.reshape(1, 1, D_MODEL).astype(_F32)
    h = x
    for l in range(DEPTH):
        first, last = l == 0, l == DEPTH - 1
        ya, qkvg = _mix_call(h, mix_consts, l, batch_major_in=first)
        yr = _ret_call(qkvg, cos2, sin2, gn_gain, l)
        h = _ffn_call(h, ya, yr, ffn_consts, fin_gain, l, batch_major_in=first, last=last)
    return h
```

```python
import functools
import math

import jax
import jax.numpy as jnp
from jax import lax
from jax.experimental import pallas as pl
from jax.experimental.pallas import tpu as pltpu

D_MODEL = 1024
BATCH = 8
SEQ = 2048
DEPTH = 2
ROWS = BATCH * SEQ

LRU_WIDTH = 512
LRU_BLOCKS = 8
LRU_CONV = 4
LRU_C = 8.0
S5_WIDTH = 512
S5_GROUP = 16
S5_GROUPS = 32
S5_STATE = 64
S5_STATES = S5_GROUPS * S5_STATE
RET_HEADS = 4
RET_HEAD_DIM = 128
RET_WIDTH = 512
ROPE_BASE = 10000.0
MIX_A_WIDTH = LRU_WIDTH + S5_WIDTH
PROJ_A_WIDTH = 2 * LRU_WIDTH + S5_WIDTH
QKVG_WIDTH = 4 * RET_WIDTH
IN_WIDTH = PROJ_A_WIDTH + QKVG_WIDTH
D_FF = 3 * D_MODEL
FFN_CONV = 3
NORM_EPS = 1e-6

SUBLANES = 8
LANES = 128
MXU_WIDTH = 256
VMEM_LIMIT_BYTES = 56 * 1024 * 1024

MIX_TILE_STEPS = 64
MIX_TILE_ROWS = MIX_TILE_STEPS * BATCH
FFN_TILE_STEPS = 64
FFN_TILE_ROWS = FFN_TILE_STEPS * BATCH
FFN_CHUNK = 1024
RET_CHUNK = 256
S5_SCAN_LANES = 512

_F32 = jnp.float32
_BF16 = jnp.bfloat16

V_CONV_W, V_CONV_B, V_BA, V_BX, V_SOFTPLUS, V_LRU_NORM, V_S5_D, V_B_GLU, V_S5_NORM = 0, 4, 5, 6, 7, 8, 9, 10, 11
V_ROWS = 16


def _gelu(x):
    return 0.5 * x * (1.0 + jnp.tanh(0.7978845608028654 * (x + 0.044715 * (x * x * x))))


def _sigmoid(x):
    return 1.0 / (1.0 + jnp.exp(-x))


def _rmsnorm(x, gain):
    return x * lax.rsqrt(jnp.mean(x * x, axis=-1, keepdims=True) + NORM_EPS) * gain


def _dot(a, b):
    return jnp.dot(a, b, preferred_element_type=_F32)


def _to_time_major(x_ref, slab, steps):
    n = x_ref.shape[-1] // LANES
    for b in range(BATCH):
        for s in range(n):
            slab[s, pl.ds(b, steps, stride=BATCH), :] = x_ref[b, :, s * LANES:(s + 1) * LANES]
    return jnp.concatenate([slab[s] for s in range(n)], axis=-1)


def _from_time_major(slab, o_ref, steps):
    n = o_ref.shape[-1] // LANES
    for b in range(BATCH):
        for s in range(n):
            o_ref[b, :, s * LANES:(s + 1) * LANES] = (
                slab[s, pl.ds(b, steps, stride=BATCH), :].astype(o_ref.dtype))


def _causal_conv(x, hist, taps, bias):
    tm = x.shape[0]
    y = bias
    for k, w in enumerate(taps):
        back = (len(taps) - 1 - k) * BATCH
        xs = x if back == 0 else jnp.concatenate([hist[hist.shape[0] - back:], x[:tm - back]], axis=0)
        y = y + w * xs
    return y


def _rotary_kernel(pos_ref, inv_ref, sign_ref, cos_ref, sin_ref):
    half = RET_HEAD_DIM // 2
    pairs = SEQ // 2
    low = lax.broadcasted_iota(jnp.int32, (pairs, RET_HEAD_DIM), 1) < half
    pos = jnp.where(low, pos_ref[0, :, 0:1], pos_ref[0, :, 1:2]).astype(_F32)
    ang = pos * inv_ref[...]
    for table, out_ref, scale in ((jnp.cos(ang), cos_ref, None), (jnp.sin(ang), sin_ref, sign_ref[...])):
        swapped = pltpu.roll(table, half, 1)
        even = jnp.where(low, table, swapped)
        odd = jnp.where(low, swapped, table)
        if scale is not None:
            even, odd = even * scale, odd * scale
        out_ref[0, pl.ds(0, pairs, stride=2), :] = even
        out_ref[0, pl.ds(1, pairs, stride=2), :] = odd


def _rotary_tables(positions):
    half = RET_HEAD_DIM // 2
    inv = ROPE_BASE ** (-jnp.arange(half, dtype=_F32) * 2.0 / RET_HEAD_DIM)
    inv2 = jnp.concatenate([inv, inv])[None, :]
    sign = jnp.concatenate([-jnp.ones((half,), _F32), jnp.ones((half,), _F32)])[None, :]
    pos3 = positions.reshape(BATCH, SEQ // 2, 2)
    out = jax.ShapeDtypeStruct((BATCH, SEQ, RET_HEAD_DIM), _F32)
    return pl.pallas_call(
        _rotary_kernel,
        grid=(BATCH,),
        in_specs=[pl.BlockSpec((1, SEQ // 2, 2), lambda b: (b, 0, 0)),
                  pl.BlockSpec((1, RET_HEAD_DIM), lambda b: (0, 0)),
                  pl.BlockSpec((1, RET_HEAD_DIM), lambda b: (0, 0))],
        out_specs=[pl.BlockSpec((1, SEQ, RET_HEAD_DIM), lambda b: (b, 0, 0)),
                   pl.BlockSpec((1, SEQ, RET_HEAD_DIM), lambda b: (b, 0, 0))],
        out_shape=[out, out],
        name="rotary_tables",
    )(pos3, inv2, sign)


def _mix_kernel(h_ref, gain_ref, w_in_ref, vec_ref, wa_ref, wx_ref,
                abar_ref, wbre_ref, wbim_ref, cre_ref, cim_ref, wglu_ref,
                ya_ref, qkvg_ref,
                conv_hist, gate_s, u_s, la_s, lb_s, lru_state, sr, si, s5_state_r, s5_state_i, qslab, hslab,
                *, batch_major_in):
    tm = MIX_TILE_ROWS
    vec = lambda row: vec_ref[row:row + 1, :]

    @pl.when(pl.program_id(0) == 0)
    def _init():
        conv_hist[...] = jnp.zeros_like(conv_hist)
        lru_state[...] = jnp.zeros_like(lru_state)
        s5_state_r[...] = jnp.zeros_like(s5_state_r)
        s5_state_i[...] = jnp.zeros_like(s5_state_i)

    h = _to_time_major(h_ref, hslab, MIX_TILE_STEPS) if batch_major_in else h_ref[...]
    xn = _rmsnorm(h, gain_ref[...]).astype(_BF16)

    lru_x = _dot(xn, w_in_ref[:, 0:LRU_WIDTH])
    u_s[...] = _dot(xn, w_in_ref[:, 2 * LRU_WIDTH:PROJ_A_WIDTH])

    xc = _causal_conv(lru_x, conv_hist[...], [vec(V_CONV_W + k) for k in range(LRU_CONV)], vec(V_CONV_B))
    conv_hist[...] = lru_x[tm - (LRU_CONV - 1) * BATCH:]
    xcb = xc.astype(_BF16)
    halves = range(LRU_WIDTH // MXU_WIDTH)
    pre_r = jnp.concatenate([_dot(xcb[:, MXU_WIDTH * p:MXU_WIDTH * (p + 1)], wa_ref[p]) for p in halves], axis=-1)
    pre_i = jnp.concatenate([_dot(xcb[:, MXU_WIDTH * p:MXU_WIDTH * (p + 1)], wx_ref[p]) for p in halves], axis=-1)

    ub = u_s[...].astype(_BF16)
    for j in range(S5_STATES // MXU_WIDTH):
        slab = ub[:, LANES * (j // 2):LANES * (j // 2 + 1)]
        sr[:, MXU_WIDTH * j:MXU_WIDTH * (j + 1)] = _dot(slab, wbre_ref[j])
        si[:, MXU_WIDTH * j:MXU_WIDTH * (j + 1)] = _dot(slab, wbim_ref[j])

    r = _sigmoid(pre_r + vec(V_BA))
    i = _sigmoid(pre_i + vec(V_BX))
    log_a = (-LRU_C) * r * vec(V_SOFTPLUS)
    a = jnp.exp(log_a)
    la_s[...] = a
    z = -jnp.tanh(log_a) * (1.0 + a * a)
    lb_s[...] = jnp.where(z == 0.0, 0.0, z * lax.rsqrt(z)) * (i * xc)

    for sg in range(S5_STATES // S5_SCAN_LANES):
        sl = slice(sg * S5_SCAN_LANES, (sg + 1) * S5_SCAN_LANES)
        ar = jnp.broadcast_to(abar_ref[0:1, sl], (BATCH, S5_SCAN_LANES))
        ai = jnp.broadcast_to(abar_ref[1:2, sl], (BATCH, S5_SCAN_LANES))
        xr = s5_state_r[:, sl]
        xi = s5_state_i[:, sl]
        for t in range(MIX_TILE_STEPS):
            rows = slice(t * BATCH, (t + 1) * BATCH)
            xr, xi = (ar * xr - ai * xi + sr[rows, sl], ar * xi + ai * xr + si[rows, sl])
            sr[rows, sl] = xr
            si[rows, sl] = xi
        s5_state_r[:, sl] = xr
        s5_state_i[:, sl] = xi

    def project_qkvg(j):
        res = _dot(xn, w_in_ref[:, PROJ_A_WIDTH + j * 512:PROJ_A_WIDTH + (j + 1) * 512])
        for s in range(512 // LANES):
            qslab[j * (512 // LANES) + s] = res[:, s * LANES:(s + 1) * LANES]

    gate_s[...] = _dot(xn, w_in_ref[:, LRU_WIDTH:2 * LRU_WIDTH])
    project_qkvg(0)
    project_qkvg(1)

    hs = lru_state[...]
    for t in range(MIX_TILE_STEPS):
        rows = slice(t * BATCH, (t + 1) * BATCH)
        hs = la_s[rows, :] * hs + lb_s[rows, :]
        lb_s[rows, :] = hs
    lru_state[...] = hs

    ys = []
    for s in range(S5_WIDTH // LANES):
        ksl = slice(512 * s, 512 * (s + 1))
        ys.append(_dot(sr[:, ksl].astype(_BF16), cre_ref[s])
                  - _dot(si[:, ksl].astype(_BF16), cim_ref[s]))
    y = jnp.concatenate(ys, axis=-1) + vec(V_S5_D) * u_s[...]
    z = _gelu(y)
    glu = _dot(z.astype(_BF16), wglu_ref[...])

    y_lru = lb_s[...] * _gelu(gate_s[...])
    ya_ref[:, 0:LRU_WIDTH] = _rmsnorm(y_lru, vec(V_LRU_NORM)).astype(ya_ref.dtype)
    out = z * _sigmoid(glu + vec(V_B_GLU))
    ya_ref[:, LRU_WIDTH:MIX_A_WIDTH] = _rmsnorm(out, vec(V_S5_NORM)).astype(ya_ref.dtype)
    project_qkvg(2)
    project_qkvg(3)
    _from_time_major(qslab, qkvg_ref, MIX_TILE_STEPS)


def _layer_spec(arr, l):
    nd = arr.ndim
    return pl.BlockSpec((None,) + arr.shape[1:], lambda *_: (l,) + (0,) * (nd - 1),
                        pipeline_mode=pl.Buffered(1))


def _stream_spec(batch_major, steps, width):
    if batch_major:
        return pl.BlockSpec((BATCH, steps, width), lambda i: (0, i, 0))
    return pl.BlockSpec((steps * BATCH, width), lambda i: (i, 0))


def _mix_call(h, consts, l, batch_major_in):
    tm = MIX_TILE_ROWS
    return pl.pallas_call(
        functools.partial(_mix_kernel, batch_major_in=batch_major_in),
        grid=(ROWS // tm,),
        in_specs=[_stream_spec(batch_major_in, MIX_TILE_STEPS, D_MODEL)] + [_layer_spec(c, l) for c in consts],
        out_specs=[pl.BlockSpec((tm, MIX_A_WIDTH), lambda i: (i, 0)),
                   _stream_spec(True, MIX_TILE_STEPS, QKVG_WIDTH)],
        out_shape=[jax.ShapeDtypeStruct((ROWS, MIX_A_WIDTH), _BF16),
                   jax.ShapeDtypeStruct((BATCH, SEQ, QKVG_WIDTH), _BF16)],
        scratch_shapes=[pltpu.VMEM(((LRU_CONV - 1) * BATCH, LRU_WIDTH), _F32),
                        pltpu.VMEM((tm, LRU_WIDTH), _F32),
                        pltpu.VMEM((tm, S5_WIDTH), _F32),
                        pltpu.VMEM((tm, LRU_WIDTH), _F32),
                        pltpu.VMEM((tm, LRU_WIDTH), _F32),
                        pltpu.VMEM((BATCH, LRU_WIDTH), _F32),
                        pltpu.VMEM((tm, S5_STATES), _F32),
                        pltpu.VMEM((tm, S5_STATES), _F32),
                        pltpu.VMEM((BATCH, S5_STATES), _F32),
                        pltpu.VMEM((BATCH, S5_STATES), _F32),
                        pltpu.VMEM((QKVG_WIDTH // LANES, tm, LANES), _F32),
                        pltpu.VMEM((D_MODEL // LANES, tm, LANES), _F32)],
        compiler_params=pltpu.CompilerParams(dimension_semantics=("arbitrary",),
                                             vmem_limit_bytes=VMEM_LIMIT_BYTES),
        name="mix_lru_s5",
    )(h, *consts)


def _ret_kernel(qkvg_ref, cos_ref, sin_ref, gn_ref, o_ref,
                state_ref, decay_ref, qdec_ref, kdec_ref):
    C = RET_CHUNK
    Dh = RET_HEAD_DIM
    log_gammas = [math.log1p(-(2.0 ** (-5.0 - h))) for h in range(RET_HEADS)]

    @pl.when((pl.program_id(0) == 0) & (pl.program_id(1) == 0))
    def _tables():
        row = lax.broadcasted_iota(jnp.int32, (C, C), 0)
        col = lax.broadcasted_iota(jnp.int32, (C, C), 1)
        rel = (row - col).astype(_F32)
        idx = lax.broadcasted_iota(jnp.int32, (C, Dh), 0).astype(_F32)
        for h in range(RET_HEADS):
            lg = log_gammas[h]
            decay_ref[h] = jnp.where(rel >= 0.0, jnp.exp(lg * jnp.maximum(rel, 0.0)), 0.0) * (Dh ** -0.5)
            qdec_ref[h] = jnp.exp(lg * (idx + 1.0))
            kdec_ref[h] = jnp.exp(lg * (C - 1.0 - idx)) * (Dh ** -0.5)

    @pl.when(pl.program_id(1) == 0)
    def _init():
        state_ref[...] = jnp.zeros_like(state_ref)

    cos2 = cos_ref[0]
    sin2 = sin_ref[0]

    def rot(t):
        return t * cos2 + pltpu.roll(t, Dh // 2, 1) * sin2

    for h in range(RET_HEADS):
        hs = slice(h * Dh, (h + 1) * Dh)
        qh = rot(qkvg_ref[:, h * Dh:(h + 1) * Dh].astype(_F32))
        kh = rot(qkvg_ref[:, RET_WIDTH + h * Dh:RET_WIDTH + (h + 1) * Dh].astype(_F32))
        vb = qkvg_ref[:, 2 * RET_WIDTH + h * Dh:2 * RET_WIDTH + (h + 1) * Dh]
        scores = lax.dot_general(qh.astype(_BF16), kh.astype(_BF16), (((1,), (1,)), ((), ())),
                                 preferred_element_type=_F32) * decay_ref[h]
        intra = _dot(scores.astype(_BF16), vb)
        state = state_ref[h]
        cross = _dot((qh * qdec_ref[h]).astype(_BF16), state.astype(_BF16))
        kv = lax.dot_general((kh * kdec_ref[h]).astype(_BF16), vb, (((0,), (0,)), ((), ())),
                             preferred_element_type=_F32)
        state_ref[h] = math.exp(log_gammas[h] * C) * state + kv
        o = intra + cross
        mu = jnp.mean(o, axis=-1, keepdims=True)
        oc = o - mu
        var = jnp.mean(oc * oc, axis=-1, keepdims=True)
        on = oc * lax.rsqrt(var + NORM_EPS) * gn_ref[:, hs]
        g = qkvg_ref[:, 3 * RET_WIDTH + h * Dh:3 * RET_WIDTH + (h + 1) * Dh].astype(_F32)
        o_ref[:, hs] = (on * (g * _sigmoid(g))).astype(o_ref.dtype)


def _ret_call(qkvg, cos2, sin2, gn_gain, l):
    C = RET_CHUNK
    return pl.pallas_call(
        _ret_kernel,
        grid=(BATCH, SEQ // C),
        in_specs=[pl.BlockSpec((None, C, QKVG_WIDTH), lambda b, n: (b, n, 0)),
                  pl.BlockSpec((1, C, RET_HEAD_DIM), lambda b, n: (b, n, 0)),
                  pl.BlockSpec((1, C, RET_HEAD_DIM), lambda b, n: (b, n, 0)),
                  pl.BlockSpec((None, 1, RET_WIDTH), lambda b, n: (l, 0, 0))],
        out_specs=pl.BlockSpec((None, C, RET_WIDTH), lambda b, n: (b, n, 0)),
        out_shape=jax.ShapeDtypeStruct((BATCH, SEQ, RET_WIDTH), _F32),
        scratch_shapes=[pltpu.VMEM((RET_HEADS, RET_HEAD_DIM, RET_HEAD_DIM), _F32),
                        pltpu.VMEM((RET_HEADS, C, C), _F32),
                        pltpu.VMEM((RET_HEADS, C, RET_HEAD_DIM), _F32),
                        pltpu.VMEM((RET_HEADS, C, RET_HEAD_DIM), _F32)],
        compiler_params=pltpu.CompilerParams(dimension_semantics=("arbitrary", "arbitrary"),
                                             vmem_limit_bytes=VMEM_LIMIT_BYTES),
        name="retention",
    )(qkvg, cos2, sin2, gn_gain)


def _ffn_kernel(h_ref, ya_ref, yr_ref, wo_ref, gains_ref, w_up_ref, conv_ref,
                w_down_ref, o_ref, acc, carry, yslab, hslab, *, batch_major_in, last):
    tm = FFN_TILE_ROWS
    fc = FFN_CHUNK

    @pl.when(pl.program_id(0) == 0)
    def _init():
        carry[...] = jnp.zeros_like(carry)

    h = _to_time_major(h_ref, hslab, FFN_TILE_STEPS) if batch_major_in else h_ref[...]
    yr = _to_time_major(yr_ref, yslab, FFN_TILE_STEPS).astype(_BF16)
    h1 = _dot(ya_ref[...], wo_ref[0:MIX_A_WIDTH, :]) + _dot(yr, wo_ref[MIX_A_WIDTH:, :]) + h
    acc[...] = h1
    xn = _rmsnorm(h1, gains_ref[0:1, :]).astype(_BF16)

    def up_proj(j):
        return (_dot(xn, w_up_ref[:, j * fc:(j + 1) * fc]),
                _dot(xn, w_up_ref[:, D_FF + j * fc:D_FF + (j + 1) * fc]))

    n_chunks = D_FF // fc
    ups = up_proj(0)
    for j in range(n_chunks):
        nxt = up_proj(j + 1) if j + 1 < n_chunks else None
        vs = slice(j * fc, (j + 1) * fc)
        gs = slice(D_FF + j * fc, D_FF + (j + 1) * fc)
        upv, upg = ups
        cv = _causal_conv(upv, carry[:, vs], [conv_ref[k:k + 1, vs] for k in range(FFN_CONV)],
                          conv_ref[FFN_CONV:FFN_CONV + 1, vs])
        cg = _causal_conv(upg, carry[:, gs], [conv_ref[k:k + 1, gs] for k in range(FFN_CONV)],
                          conv_ref[FFN_CONV:FFN_CONV + 1, gs])
        carry[:, vs] = upv[tm - (FFN_CONV - 1) * BATCH:]
        carry[:, gs] = upg[tm - (FFN_CONV - 1) * BATCH:]
        act = (_gelu(cg) * cv).astype(_BF16)
        down = _dot(act, w_down_ref[vs, :])
        if j + 1 < n_chunks:
            acc[...] += down
        ups = nxt
    total = acc[...] + down
    if last:
        out = _rmsnorm(total, gains_ref[1:2, :])
        for s in range(D_MODEL // LANES):
            hslab[s] = out[:, s * LANES:(s + 1) * LANES]
        _from_time_major(hslab, o_ref, FFN_TILE_STEPS)
    else:
        o_ref[...] = total


def _ffn_call(h, ya, yr, consts, l, batch_major_in, last):
    tm = FFN_TILE_ROWS
    out_shape = (BATCH, SEQ, D_MODEL) if last else (ROWS, D_MODEL)
    return pl.pallas_call(
        functools.partial(_ffn_kernel, batch_major_in=batch_major_in, last=last),
        grid=(ROWS // tm,),
        in_specs=[_stream_spec(batch_major_in, FFN_TILE_STEPS, D_MODEL),
                  pl.BlockSpec((tm, MIX_A_WIDTH), lambda i: (i, 0)),
                  _stream_spec(True, FFN_TILE_STEPS, RET_WIDTH)]
                 + [_layer_spec(c, l) for c in consts],
        out_specs=_stream_spec(last, FFN_TILE_STEPS, D_MODEL),
        out_shape=jax.ShapeDtypeStruct(out_shape, _F32),
        scratch_shapes=[pltpu.VMEM((tm, D_MODEL), _F32),
                        pltpu.VMEM(((FFN_CONV - 1) * BATCH, 2 * D_FF), _F32),
                        pltpu.VMEM((RET_WIDTH // LANES, tm, LANES), _F32),
                        pltpu.VMEM((D_MODEL // LANES, tm, LANES), _F32)],
        compiler_params=pltpu.CompilerParams(dimension_semantics=("arbitrary",),
                                             vmem_limit_bytes=VMEM_LIMIT_BYTES),
        name="outproj_ffn",
    )(h, ya, yr, *consts)


def _pack_gate(w):
    blk = LRU_WIDTH // LRU_BLOCKS
    per = MXU_WIDTH // blk
    w5 = w.astype(_F32).reshape(DEPTH, LRU_BLOCKS // per, per, blk, blk)
    t = jnp.einsum('lphij,hg->lphigj', w5, jnp.eye(per, dtype=_F32))
    return t.reshape(DEPTH, LRU_BLOCKS // per, MXU_WIDTH, MXU_WIDTH).astype(_BF16)


def _s5_params(lam_re, lam_im, log_dt, b_re, b_im, c_re, c_im):
    dt = jnp.exp(log_dt.astype(_F32))[..., None]
    lr, li = lam_re.astype(_F32), lam_im.astype(_F32)
    mag = jnp.exp(lr * dt)
    abar_re, abar_im = mag * jnp.cos(li * dt), mag * jnp.sin(li * dt)
    den = lr * lr + li * li
    nr, ni = abar_re - 1.0, abar_im
    coef_re = ((nr * lr + ni * li) / den)[..., None]
    coef_im = ((ni * lr - nr * li) / den)[..., None]
    br, bi = b_re.astype(_F32), b_im.astype(_F32)
    bbar_re = coef_re * br - coef_im * bi
    bbar_im = coef_re * bi + coef_im * br

    n_bt = S5_STATES // MXU_WIDTH
    g_out = MXU_WIDTH // S5_STATE
    g_in = LANES // S5_GROUP
    sel = (jnp.arange(g_in)[None, :, None]
           == (g_out * (jnp.arange(n_bt) % 2))[:, None, None] + jnp.arange(g_out)[None, None, :]).astype(_F32)

    def pack_b(bbar):
        bb = bbar.reshape(DEPTH, n_bt, g_out, S5_STATE, S5_GROUP)
        t = jnp.einsum('ljopc,jio->ljicop', bb, sel)
        return t.reshape(DEPTH, n_bt, LANES, MXU_WIDTH).astype(_BF16)

    def pack_c(c):
        n_sl = S5_WIDTH // LANES
        cc = c.astype(_F32).reshape(DEPTH, n_sl, g_in, S5_GROUP, S5_STATE)
        t = jnp.einsum('lsicp,io->lsipoc', cc, jnp.eye(g_in, dtype=_F32))
        return t.reshape(DEPTH, n_sl, g_in * S5_STATE, LANES).astype(_BF16)

    abar = jnp.stack([abar_re.reshape(DEPTH, S5_STATES), abar_im.reshape(DEPTH, S5_STATES)], axis=1)
    return (abar, pack_b(bbar_re), pack_b(bbar_im), pack_c(c_re), pack_c(c_im))


def kernel(x, positions, norm_mix, w_in, lru_conv_w, lru_conv_b, lru_wa, lru_ba, lru_wx, lru_bx, lru_lambda, lru_norm, s5_lambda_re, s5_lambda_im, s5_log_dt, s5_b_re, s5_b_im, s5_c_re, s5_c_im, s5_d, s5_w_glu, s5_b_glu, s5_norm, ret_norm, w_out, norm_ffn, w_up, ffn_conv_w, ffn_conv_b, w_down, norm_final):
    cos2, sin2 = _rotary_tables(positions)
    f32 = lambda v: v.astype(_F32)
    vec512 = jnp.concatenate(
        [f32(lru_conv_w),
         jnp.stack([f32(lru_conv_b), f32(lru_ba).reshape(DEPTH, LRU_WIDTH), f32(lru_bx).reshape(DEPTH, LRU_WIDTH),
                    jax.nn.softplus(-f32(lru_lambda)), f32(lru_norm), f32(s5_d), f32(s5_b_glu), f32(s5_norm)], axis=1),
         jnp.zeros((DEPTH, V_ROWS - V_S5_NORM - 1, LRU_WIDTH), _F32)], axis=1)
    abar, wb_re, wb_im, c_re, c_im = _s5_params(s5_lambda_re, s5_lambda_im, s5_log_dt, s5_b_re, s5_b_im,
                                                s5_c_re, s5_c_im)
    mix_consts = [f32(norm_mix).reshape(DEPTH, 1, D_MODEL), w_in.astype(_BF16), vec512,
                  _pack_gate(lru_wa), _pack_gate(lru_wx), abar, wb_re, wb_im, c_re, c_im, s5_w_glu.astype(_BF16)]
    gains = jnp.stack([f32(norm_ffn), jnp.broadcast_to(f32(norm_final), (DEPTH, D_MODEL))], axis=1)
    ffn_conv = jnp.concatenate([f32(ffn_conv_w), f32(ffn_conv_b)[:, None, :]], axis=1)
    ffn_consts = [w_out.astype(_BF16), gains, w_up.astype(_BF16), ffn_conv, w_down.astype(_BF16)]
    gn_gain = f32(ret_norm).reshape(DEPTH, 1, RET_WIDTH)
    h = x
    for l in range(DEPTH):
        first, last = l == 0, l == DEPTH - 1
        ya, qkvg = _mix_call(h, mix_consts, l, batch_major_in=first)
        yr = _ret_call(qkvg, cos2, sin2, gn_gain, l)
        h = _ffn_call(h, ya, yr, ffn_consts, l, batch_major_in=first, last=last)
    return h
```

```python
import functools
import math

import jax
import jax.numpy as jnp
from jax import lax
from jax.experimental import pallas as pl
from jax.experimental.pallas import tpu as pltpu

D_MODEL = 1024
BATCH = 8
SEQ = 2048
DEPTH = 2
ROWS = BATCH * SEQ

LRU_WIDTH = 512
LRU_BLOCKS = 8
LRU_CONV = 4
LRU_C = 8.0
S5_WIDTH = 512
S5_GROUP = 16
S5_GROUPS = 32
S5_STATE = 64
S5_STATES = S5_GROUPS * S5_STATE
RET_HEADS = 4
RET_HEAD_DIM = 128
RET_WIDTH = 512
ROPE_BASE = 10000.0
MIX_A_WIDTH = LRU_WIDTH + S5_WIDTH
PROJ_A_WIDTH = 2 * LRU_WIDTH + S5_WIDTH
QKVG_WIDTH = 4 * RET_WIDTH
IN_WIDTH = PROJ_A_WIDTH + QKVG_WIDTH
D_FF = 3 * D_MODEL
FFN_CONV = 3
NORM_EPS = 1e-6

SUBLANES = 8
LANES = 128
MXU_WIDTH = 256
VMEM_LIMIT_BYTES = 56 * 1024 * 1024

MIX_TILE_STEPS = 64
MIX_TILE_ROWS = MIX_TILE_STEPS * BATCH
FFN_TILE_STEPS = 64
FFN_TILE_ROWS = FFN_TILE_STEPS * BATCH
FFN_CHUNK = 1024
RET_CHUNK = 256
S5_SCAN_LANES = 512
S5_BLOCK = 8

_F32 = jnp.float32
_BF16 = jnp.bfloat16

V_CONV_W, V_CONV_B, V_BA, V_BX, V_SOFTPLUS, V_LRU_NORM, V_S5_D, V_B_GLU, V_S5_NORM = 0, 4, 5, 6, 7, 8, 9, 10, 11
V_ROWS = 16


def _gelu(x):
    return 0.5 * x * (1.0 + jnp.tanh(0.7978845608028654 * (x + 0.044715 * (x * x * x))))


def _sigmoid(x):
    return 1.0 / (1.0 + jnp.exp(-x))


def _rmsnorm(x, gain):
    return x * lax.rsqrt(jnp.mean(x * x, axis=-1, keepdims=True) + NORM_EPS) * gain


def _dot(a, b):
    return jnp.dot(a, b, preferred_element_type=_F32)


def _to_time_major(x_ref, slab, steps):
    n = x_ref.shape[-1] // LANES
    for b in range(BATCH):
        for s in range(n):
            slab[s, pl.ds(b, steps, stride=BATCH), :] = x_ref[b, :, s * LANES:(s + 1) * LANES]
    return jnp.concatenate([slab[s] for s in range(n)], axis=-1)


def _from_time_major(slab, o_ref, steps):
    n = o_ref.shape[-1] // LANES
    for b in range(BATCH):
        for s in range(n):
            o_ref[b, :, s * LANES:(s + 1) * LANES] = (
                slab[s, pl.ds(b, steps, stride=BATCH), :].astype(o_ref.dtype))


def _causal_conv(x, hist, taps, bias):
    tm = x.shape[0]
    y = bias
    for k, w in enumerate(taps):
        back = (len(taps) - 1 - k) * BATCH
        xs = x if back == 0 else jnp.concatenate([hist[hist.shape[0] - back:], x[:tm - back]], axis=0)
        y = y + w * xs
    return y


def _rotary_kernel(pos_ref, inv_ref, sign_ref, cos_ref, sin_ref):
    half = RET_HEAD_DIM // 2
    pairs = SEQ // 2
    low = lax.broadcasted_iota(jnp.int32, (pairs, RET_HEAD_DIM), 1) < half
    pos = jnp.where(low, pos_ref[0, :, 0:1], pos_ref[0, :, 1:2]).astype(_F32)
    ang = pos * inv_ref[...]
    for table, out_ref, scale in ((jnp.cos(ang), cos_ref, None), (jnp.sin(ang), sin_ref, sign_ref[...])):
        swapped = pltpu.roll(table, half, 1)
        even = jnp.where(low, table, swapped)
        odd = jnp.where(low, swapped, table)
        if scale is not None:
            even, odd = even * scale, odd * scale
        out_ref[0, pl.ds(0, pairs, stride=2), :] = even
        out_ref[0, pl.ds(1, pairs, stride=2), :] = odd


def _rotary_tables(positions):
    half = RET_HEAD_DIM // 2
    inv = ROPE_BASE ** (-jnp.arange(half, dtype=_F32) * 2.0 / RET_HEAD_DIM)
    inv2 = jnp.concatenate([inv, inv])[None, :]
    sign = jnp.concatenate([-jnp.ones((half,), _F32), jnp.ones((half,), _F32)])[None, :]
    pos3 = positions.reshape(BATCH, SEQ // 2, 2)
    out = jax.ShapeDtypeStruct((BATCH, SEQ, RET_HEAD_DIM), _F32)
    return pl.pallas_call(
        _rotary_kernel,
        grid=(BATCH,),
        in_specs=[pl.BlockSpec((1, SEQ // 2, 2), lambda b: (b, 0, 0)),
                  pl.BlockSpec((1, RET_HEAD_DIM), lambda b: (0, 0)),
                  pl.BlockSpec((1, RET_HEAD_DIM), lambda b: (0, 0))],
        out_specs=[pl.BlockSpec((1, SEQ, RET_HEAD_DIM), lambda b: (b, 0, 0)),
                   pl.BlockSpec((1, SEQ, RET_HEAD_DIM), lambda b: (b, 0, 0))],
        out_shape=[out, out],
        name="rotary_tables",
    )(pos3, inv2, sign)


def _mix_kernel(h_ref, gain_ref, w_in_ref, vec_ref, wa_ref, wx_ref,
                a8_ref, w1_ref, w2_ref, w3_ref, wglu_ref,
                ya_ref, qkvg_ref,
                conv_hist, gate_s, u_s, la_s, lb_s, lru_state, sr, si, y_s, s5_state_r, s5_state_i, qslab, hslab,
                *, batch_major_in):
    tm = MIX_TILE_ROWS
    vec = lambda row: vec_ref[row:row + 1, :]

    @pl.when(pl.program_id(0) == 0)
    def _init():
        conv_hist[...] = jnp.zeros_like(conv_hist)
        lru_state[...] = jnp.zeros_like(lru_state)
        s5_state_r[...] = jnp.zeros_like(s5_state_r)
        s5_state_i[...] = jnp.zeros_like(s5_state_i)

    h = _to_time_major(h_ref, hslab, MIX_TILE_STEPS) if batch_major_in else h_ref[...]
    xn = _rmsnorm(h, gain_ref[...]).astype(_BF16)

    lru_x = _dot(xn, w_in_ref[:, 0:LRU_WIDTH])
    u_s[...] = _dot(xn, w_in_ref[:, 2 * LRU_WIDTH:PROJ_A_WIDTH])

    xc = _causal_conv(lru_x, conv_hist[...], [vec(V_CONV_W + k) for k in range(LRU_CONV)], vec(V_CONV_B))
    conv_hist[...] = lru_x[tm - (LRU_CONV - 1) * BATCH:]
    xcb = xc.astype(_BF16)
    halves = range(LRU_WIDTH // MXU_WIDTH)
    pre_r = jnp.concatenate([_dot(xcb[:, MXU_WIDTH * p:MXU_WIDTH * (p + 1)], wa_ref[p]) for p in halves], axis=-1)
    pre_i = jnp.concatenate([_dot(xcb[:, MXU_WIDTH * p:MXU_WIDTH * (p + 1)], wx_ref[p]) for p in halves], axis=-1)

    nb = MIX_TILE_STEPS // S5_BLOCK
    gps = LANES // S5_GROUP
    lane_blk = lax.broadcasted_iota(jnp.int32, (nb * BATCH, LANES), 1) // S5_GROUP
    u = u_s[...]
    lag_rows = [jnp.concatenate([u[(S5_BLOCK * k + i) * BATCH:(S5_BLOCK * k + i + 1) * BATCH, :]
                                 for k in range(nb)], axis=0) for i in range(S5_BLOCK)]
    u_grp = []
    for sb in range(S5_WIDTH // LANES):
        rot = [lag_rows[i][:, sb * LANES:(sb + 1) * LANES] for i in range(S5_BLOCK)]
        rot = [r if i == 0 else pltpu.roll(r, i * S5_GROUP, 1) for i, r in enumerate(rot)]
        rot = [r.astype(_BF16) for r in rot]
        for q in range(gps):
            o = rot[S5_BLOCK - 1]
            for i in range(S5_BLOCK - 2, -1, -1):
                o = jnp.where(lane_blk == (q + i) % gps, rot[i], o)
            u_grp.append(o)
    n_pair = S5_GROUPS // 2
    u_pair = [jnp.concatenate([u_grp[2 * m], u_grp[2 * m + 1]], axis=1) for m in range(n_pair)]

    for m in range(n_pair):
        v = _dot(u_pair[m], w1_ref[m])
        sr[:, LANES * m:LANES * (m + 1)] = v[:, 0:LANES]
        si[:, LANES * m:LANES * (m + 1)] = v[:, LANES:2 * LANES]

    r = _sigmoid(pre_r + vec(V_BA))
    i = _sigmoid(pre_i + vec(V_BX))
    log_a = (-LRU_C) * r * vec(V_SOFTPLUS)
    a = jnp.exp(log_a)
    la_s[...] = a
    z = -jnp.tanh(log_a) * (1.0 + a * a)
    lb_s[...] = jnp.where(z == 0.0, 0.0, z * lax.rsqrt(z)) * (i * xc)

    for sg in range(S5_STATES // S5_SCAN_LANES):
        sl = slice(sg * S5_SCAN_LANES, (sg + 1) * S5_SCAN_LANES)
        ar = jnp.broadcast_to(a8_ref[0:1, sl], (BATCH, S5_SCAN_LANES))
        ai = jnp.broadcast_to(a8_ref[1:2, sl], (BATCH, S5_SCAN_LANES))
        xr = s5_state_r[:, sl]
        xi = s5_state_i[:, sl]
        for k in range(nb):
            rows = slice(k * BATCH, (k + 1) * BATCH)
            vr, vi = sr[rows, sl], si[rows, sl]
            sr[rows, sl] = xr
            si[rows, sl] = xi
            xr, xi = ar * xr - ai * xi + vr, ar * xi + ai * xr + vi
        s5_state_r[:, sl] = xr
        s5_state_i[:, sl] = xi

    def project_qkvg(j):
        res = _dot(xn, w_in_ref[:, PROJ_A_WIDTH + j * 512:PROJ_A_WIDTH + (j + 1) * 512])
        for s in range(512 // LANES):
            qslab[j * (512 // LANES) + s] = res[:, s * LANES:(s + 1) * LANES]

    gate_s[...] = _dot(xn, w_in_ref[:, LRU_WIDTH:2 * LRU_WIDTH])
    project_qkvg(0)
    project_qkvg(1)

    hs = lru_state[...]
    for t in range(MIX_TILE_STEPS):
        rows = slice(t * BATCH, (t + 1) * BATCH)
        hs = la_s[rows, :] * hs + lb_s[rows, :]
        lb_s[rows, :] = hs
    lru_state[...] = hs

    y_grp = []
    for m in range(n_pair):
        s_cat = jnp.concatenate([sr[:, LANES * m:LANES * (m + 1)], si[:, LANES * m:LANES * (m + 1)]],
                                axis=1).astype(_BF16)
        ym = _dot(s_cat, w2_ref[m]) + _dot(u_pair[m], w3_ref[m])
        y_grp += [ym[:, 0:LANES], ym[:, LANES:2 * LANES]]
    for sb in range(S5_WIDTH // LANES):
        for j in range(S5_BLOCK):
            t = y_grp[sb * gps + gps - 1]
            for q in range(gps - 2, -1, -1):
                t = jnp.where(lane_blk == (q + j) % gps, y_grp[sb * gps + q], t)
            if j:
                t = pltpu.roll(t, LANES - j * S5_GROUP, 1)
            for k in range(nb):
                row = (S5_BLOCK * k + j) * BATCH
                y_s[row:row + BATCH, sb * LANES:(sb + 1) * LANES] = t[k * BATCH:(k + 1) * BATCH, :]
    y = y_s[...] + vec(V_S5_D) * u_s[...]
    z = _gelu(y)
    glu = _dot(z.astype(_BF16), wglu_ref[...])

    y_lru = lb_s[...] * _gelu(gate_s[...])
    ya_ref[:, 0:LRU_WIDTH] = _rmsnorm(y_lru, vec(V_LRU_NORM)).astype(ya_ref.dtype)
    out = z * _sigmoid(glu + vec(V_B_GLU))
    ya_ref[:, LRU_WIDTH:MIX_A_WIDTH] = _rmsnorm(out, vec(V_S5_NORM)).astype(ya_ref.dtype)
    project_qkvg(2)
    project_qkvg(3)
    _from_time_major(qslab, qkvg_ref, MIX_TILE_STEPS)


def _layer_spec(arr, l):
    nd = arr.ndim
    return pl.BlockSpec((None,) + arr.shape[1:], lambda *_: (l,) + (0,) * (nd - 1),
                        pipeline_mode=pl.Buffered(1))


def _stream_spec(batch_major, steps, width):
    if batch_major:
        return pl.BlockSpec((BATCH, steps, width), lambda i: (0, i, 0))
    return pl.BlockSpec((steps * BATCH, width), lambda i: (i, 0))


def _mix_call(h, consts, l, batch_major_in):
    tm = MIX_TILE_ROWS
    return pl.pallas_call(
        functools.partial(_mix_kernel, batch_major_in=batch_major_in),
        grid=(ROWS // tm,),
        in_specs=[_stream_spec(batch_major_in, MIX_TILE_STEPS, D_MODEL)] + [_layer_spec(c, l) for c in consts],
        out_specs=[pl.BlockSpec((tm, MIX_A_WIDTH), lambda i: (i, 0)),
                   _stream_spec(True, MIX_TILE_STEPS, QKVG_WIDTH)],
        out_shape=[jax.ShapeDtypeStruct((ROWS, MIX_A_WIDTH), _BF16),
                   jax.ShapeDtypeStruct((BATCH, SEQ, QKVG_WIDTH), _BF16)],
        scratch_shapes=[pltpu.VMEM(((LRU_CONV - 1) * BATCH, LRU_WIDTH), _F32),
                        pltpu.VMEM((tm, LRU_WIDTH), _F32),
                        pltpu.VMEM((tm, S5_WIDTH), _F32),
                        pltpu.VMEM((tm, LRU_WIDTH), _F32),
                        pltpu.VMEM((tm, LRU_WIDTH), _F32),
                        pltpu.VMEM((BATCH, LRU_WIDTH), _F32),
                        pltpu.VMEM((tm // S5_BLOCK, S5_STATES), _F32),
                        pltpu.VMEM((tm // S5_BLOCK, S5_STATES), _F32),
                        pltpu.VMEM((tm, S5_WIDTH), _F32),
                        pltpu.VMEM((BATCH, S5_STATES), _F32),
                        pltpu.VMEM((BATCH, S5_STATES), _F32),
                        pltpu.VMEM((QKVG_WIDTH // LANES, tm, LANES), _F32),
                        pltpu.VMEM((D_MODEL // LANES, tm, LANES), _F32)],
        compiler_params=pltpu.CompilerParams(dimension_semantics=("arbitrary",),
                                             vmem_limit_bytes=VMEM_LIMIT_BYTES),
        name="mix_lru_s5",
    )(h, *consts)


def _ret_kernel(qkvg_ref, cos_ref, sin_ref, gn_ref, o_ref,
                state_ref, decay_ref, qdec_ref, kdec_ref):
    C = RET_CHUNK
    Dh = RET_HEAD_DIM
    log_gammas = [math.log1p(-(2.0 ** (-5.0 - h))) for h in range(RET_HEADS)]

    @pl.when((pl.program_id(0) == 0) & (pl.program_id(1) == 0))
    def _tables():
        row = lax.broadcasted_iota(jnp.int32, (C, C), 0)
        col = lax.broadcasted_iota(jnp.int32, (C, C), 1)
        rel = (row - col).astype(_F32)
        idx = lax.broadcasted_iota(jnp.int32, (C, Dh), 0).astype(_F32)
        for h in range(RET_HEADS):
            lg = log_gammas[h]
            decay_ref[h] = jnp.where(rel >= 0.0, jnp.exp(lg * jnp.maximum(rel, 0.0)), 0.0) * (Dh ** -0.5)
            qdec_ref[h] = jnp.exp(lg * (idx + 1.0))
            kdec_ref[h] = jnp.exp(lg * (C - 1.0 - idx)) * (Dh ** -0.5)

    @pl.when(pl.program_id(1) == 0)
    def _init():
        state_ref[...] = jnp.zeros_like(state_ref)

    cos2 = cos_ref[0]
    sin2 = sin_ref[0]

    def rot(t):
        return t * cos2 + pltpu.roll(t, Dh // 2, 1) * sin2

    for h in range(RET_HEADS):
        hs = slice(h * Dh, (h + 1) * Dh)
        qh = rot(qkvg_ref[:, h * Dh:(h + 1) * Dh].astype(_F32))
        kh = rot(qkvg_ref[:, RET_WIDTH + h * Dh:RET_WIDTH + (h + 1) * Dh].astype(_F32))
        vb = qkvg_ref[:, 2 * RET_WIDTH + h * Dh:2 * RET_WIDTH + (h + 1) * Dh]
        scores = lax.dot_general(qh.astype(_BF16), kh.astype(_BF16), (((1,), (1,)), ((), ())),
                                 preferred_element_type=_F32) * decay_ref[h]
        intra = _dot(scores.astype(_BF16), vb)
        state = state_ref[h]
        cross = _dot((qh * qdec_ref[h]).astype(_BF16), state.astype(_BF16))
        kv = lax.dot_general((kh * kdec_ref[h]).astype(_BF16), vb, (((0,), (0,)), ((), ())),
                             preferred_element_type=_F32)
        state_ref[h] = math.exp(log_gammas[h] * C) * state + kv
        o = intra + cross
        mu = jnp.mean(o, axis=-1, keepdims=True)
        oc = o - mu
        var = jnp.mean(oc * oc, axis=-1, keepdims=True)
        on = oc * lax.rsqrt(var + NORM_EPS) * gn_ref[:, hs]
        g = qkvg_ref[:, 3 * RET_WIDTH + h * Dh:3 * RET_WIDTH + (h + 1) * Dh].astype(_F32)
        o_ref[:, hs] = (on * (g * _sigmoid(g))).astype(o_ref.dtype)


def _ret_call(qkvg, cos2, sin2, gn_gain, l):
    C = RET_CHUNK
    return pl.pallas_call(
        _ret_kernel,
        grid=(BATCH, SEQ // C),
        in_specs=[pl.BlockSpec((None, C, QKVG_WIDTH), lambda b, n: (b, n, 0)),
                  pl.BlockSpec((1, C, RET_HEAD_DIM), lambda b, n: (b, n, 0)),
                  pl.BlockSpec((1, C, RET_HEAD_DIM), lambda b, n: (b, n, 0)),
                  pl.BlockSpec((None, 1, RET_WIDTH), lambda b, n: (l, 0, 0))],
        out_specs=pl.BlockSpec((None, C, RET_WIDTH), lambda b, n: (b, n, 0)),
        out_shape=jax.ShapeDtypeStruct((BATCH, SEQ, RET_WIDTH), _F32),
        scratch_shapes=[pltpu.VMEM((RET_HEADS, RET_HEAD_DIM, RET_HEAD_DIM), _F32),
                        pltpu.VMEM((RET_HEADS, C, C), _F32),
                        pltpu.VMEM((RET_HEADS, C, RET_HEAD_DIM), _F32),
                        pltpu.VMEM((RET_HEADS, C, RET_HEAD_DIM), _F32)],
        compiler_params=pltpu.CompilerParams(dimension_semantics=("arbitrary", "arbitrary"),
                                             vmem_limit_bytes=VMEM_LIMIT_BYTES),
        name="retention",
    )(qkvg, cos2, sin2, gn_gain)


def _ffn_kernel(h_ref, ya_ref, yr_ref, wo_ref, gains_ref, w_up_ref, conv_ref,
                w_down_ref, o_ref, acc, carry, yslab, hslab, *, batch_major_in, last):
    tm = FFN_TILE_ROWS
    fc = FFN_CHUNK

    @pl.when(pl.program_id(0) == 0)
    def _init():
        carry[...] = jnp.zeros_like(carry)

    h = _to_time_major(h_ref, hslab, FFN_TILE_STEPS) if batch_major_in else h_ref[...]
    yr = _to_time_major(yr_ref, yslab, FFN_TILE_STEPS).astype(_BF16)
    h1 = _dot(ya_ref[...], wo_ref[0:MIX_A_WIDTH, :]) + _dot(yr, wo_ref[MIX_A_WIDTH:, :]) + h
    acc[...] = h1
    xn = _rmsnorm(h1, gains_ref[0:1, :]).astype(_BF16)

    def up_proj(j):
        return (_dot(xn, w_up_ref[:, j * fc:(j + 1) * fc]),
                _dot(xn, w_up_ref[:, D_FF + j * fc:D_FF + (j + 1) * fc]))

    n_chunks = D_FF // fc
    ups = up_proj(0)
    for j in range(n_chunks):
        nxt = up_proj(j + 1) if j + 1 < n_chunks else None
        vs = slice(j * fc, (j + 1) * fc)
        gs = slice(D_FF + j * fc, D_FF + (j + 1) * fc)
        upv, upg = ups
        cv = _causal_conv(upv, carry[:, vs], [conv_ref[k:k + 1, vs] for k in range(FFN_CONV)],
                          conv_ref[FFN_CONV:FFN_CONV + 1, vs])
        cg = _causal_conv(upg, carry[:, gs], [conv_ref[k:k + 1, gs] for k in range(FFN_CONV)],
                          conv_ref[FFN_CONV:FFN_CONV + 1, gs])
        carry[:, vs] = upv[tm - (FFN_CONV - 1) * BATCH:]
        carry[:, gs] = upg[tm - (FFN_CONV - 1) * BATCH:]
        act = (_gelu(cg) * cv).astype(_BF16)
        down = _dot(act, w_down_ref[vs, :])
        if j + 1 < n_chunks:
            acc[...] += down
        ups = nxt
    total = acc[...] + down
    if last:
        out = _rmsnorm(total, gains_ref[1:2, :])
        for s in range(D_MODEL // LANES):
            hslab[s] = out[:, s * LANES:(s + 1) * LANES]
        _from_time_major(hslab, o_ref, FFN_TILE_STEPS)
    else:
        o_ref[...] = total


def _ffn_call(h, ya, yr, consts, l, batch_major_in, last):
    tm = FFN_TILE_ROWS
    out_shape = (BATCH, SEQ, D_MODEL) if last else (ROWS, D_MODEL)
    return pl.pallas_call(
        functools.partial(_ffn_kernel, batch_major_in=batch_major_in, last=last),
        grid=(ROWS // tm,),
        in_specs=[_stream_spec(batch_major_in, FFN_TILE_STEPS, D_MODEL),
                  pl.BlockSpec((tm, MIX_A_WIDTH), lambda i: (i, 0)),
                  _stream_spec(True, FFN_TILE_STEPS, RET_WIDTH)]
                 + [_layer_spec(c, l) for c in consts],
        out_specs=_stream_spec(last, FFN_TILE_STEPS, D_MODEL),
        out_shape=jax.ShapeDtypeStruct(out_shape, _F32),
        scratch_shapes=[pltpu.VMEM((tm, D_MODEL), _F32),
                        pltpu.VMEM(((FFN_CONV - 1) * BATCH, 2 * D_FF), _F32),
                        pltpu.VMEM((RET_WIDTH // LANES, tm, LANES), _F32),
                        pltpu.VMEM((D_MODEL // LANES, tm, LANES), _F32)],
        compiler_params=pltpu.CompilerParams(dimension_semantics=("arbitrary",),
                                             vmem_limit_bytes=VMEM_LIMIT_BYTES),
        name="outproj_ffn",
    )(h, ya, yr, *consts)


def _pack_gate(w):
    blk = LRU_WIDTH // LRU_BLOCKS
    per = MXU_WIDTH // blk
    w5 = w.astype(_F32).reshape(DEPTH, LRU_BLOCKS // per, per, blk, blk)
    t = jnp.einsum('lphij,hg->lphigj', w5, jnp.eye(per, dtype=_F32))
    return t.reshape(DEPTH, LRU_BLOCKS // per, MXU_WIDTH, MXU_WIDTH).astype(_BF16)


def _s5_params(lam_re, lam_im, log_dt, b_re, b_im, c_re, c_im):
    dt = jnp.exp(log_dt.astype(_F32))[..., None]
    lr, li = lam_re.astype(_F32), lam_im.astype(_F32)
    mag = jnp.exp(lr * dt)
    abar_re, abar_im = mag * jnp.cos(li * dt), mag * jnp.sin(li * dt)
    den = lr * lr + li * li
    nr, ni = abar_re - 1.0, abar_im
    coef_re = ((nr * lr + ni * li) / den)[..., None]
    coef_im = ((ni * lr - nr * li) / den)[..., None]
    br, bi = b_re.astype(_F32), b_im.astype(_F32)
    bbar_re = coef_re * br - coef_im * bi
    bbar_im = coef_re * bi + coef_im * br

    T = S5_BLOCK
    abar = lax.complex(abar_re, abar_im)
    bbar = lax.complex(bbar_re, bbar_im)
    cc = lax.complex(c_re.astype(_F32), c_im.astype(_F32))
    powers = [jnp.ones_like(abar)]
    for _ in range(T):
        powers.append(powers[-1] * abar)
    pw = jnp.stack(powers, axis=2)

    w1 = jnp.einsum('lgip,lgpc->lgicp', pw[:, :, T - 1::-1][:, :, :T], bbar)
    w2 = jnp.einsum('lgcp,lgjp->lgpjc', cc, pw[:, :, 1:T + 1])
    kt = jnp.real(jnp.einsum('lgop,lgtp,lgpi->lgtoi', cc, pw[:, :, :T], bbar))
    lag = jnp.arange(T)
    tau = lag[None, :] - lag[:, None]
    w3 = jnp.where((tau >= 0)[None, None, :, :, None, None], kt[:, :, jnp.maximum(tau, 0)], 0.0)
    w3 = w3.transpose(0, 1, 2, 5, 3, 4)

    gidx = jnp.arange(S5_GROUPS)
    lag_of = (jnp.arange(T)[None, :] - (gidx % (LANES // S5_GROUP))[:, None]) % T
    w1 = w1[:, gidx[:, None], lag_of]
    w2 = w2.transpose(0, 1, 3, 2, 4)[:, gidx[:, None], lag_of].transpose(0, 1, 3, 2, 4)
    w3 = w3[:, gidx[:, None], lag_of]
    w3 = w3.transpose(0, 1, 4, 5, 2, 3)[:, gidx[:, None], lag_of].transpose(0, 1, 4, 5, 2, 3)

    n_pair = S5_GROUPS // 2
    eye2 = jnp.eye(2, dtype=_F32)

    def pair_tiles(mat):
        r, c = mat.shape[-2:]
        t = jnp.einsum('lmarc,ab->lmarbc', mat.reshape(DEPTH, n_pair, 2, r, c), eye2)
        return t.reshape(DEPTH, n_pair, 2 * r, 2 * c)

    w1 = w1.reshape(DEPTH, S5_GROUPS, T * S5_GROUP, S5_STATE)
    w1_t = jnp.concatenate([pair_tiles(jnp.real(w1)), pair_tiles(jnp.imag(w1))], axis=-1)
    w2 = w2.reshape(DEPTH, S5_GROUPS, S5_STATE, T * S5_GROUP)
    w2_t = jnp.concatenate([pair_tiles(jnp.real(w2)), pair_tiles(-jnp.imag(w2))], axis=-2)
    w3_t = pair_tiles(w3.reshape(DEPTH, S5_GROUPS, T * S5_GROUP, T * S5_GROUP))
    a_blk = pw[:, :, T]
    a8 = jnp.stack([jnp.real(a_blk).reshape(DEPTH, S5_STATES), jnp.imag(a_blk).reshape(DEPTH, S5_STATES)], axis=1)
    return a8, w1_t.astype(_BF16), w2_t.astype(_BF16), w3_t.astype(_BF16)


def kernel(x, positions, norm_mix, w_in, lru_conv_w, lru_conv_b, lru_wa, lru_ba, lru_wx, lru_bx, lru_lambda, lru_norm, s5_lambda_re, s5_lambda_im, s5_log_dt, s5_b_re, s5_b_im, s5_c_re, s5_c_im, s5_d, s5_w_glu, s5_b_glu, s5_norm, ret_norm, w_out, norm_ffn, w_up, ffn_conv_w, ffn_conv_b, w_down, norm_final):
    cos2, sin2 = _rotary_tables(positions)
    f32 = lambda v: v.astype(_F32)
    vec512 = jnp.concatenate(
        [f32(lru_conv_w),
         jnp.stack([f32(lru_conv_b), f32(lru_ba).reshape(DEPTH, LRU_WIDTH), f32(lru_bx).reshape(DEPTH, LRU_WIDTH),
                    jax.nn.softplus(-f32(lru_lambda)), f32(lru_norm), f32(s5_d), f32(s5_b_glu), f32(s5_norm)], axis=1),
         jnp.zeros((DEPTH, V_ROWS - V_S5_NORM - 1, LRU_WIDTH), _F32)], axis=1)
    a8, w1, w2, w3 = _s5_params(s5_lambda_re, s5_lambda_im, s5_log_dt, s5_b_re, s5_b_im, s5_c_re, s5_c_im)
    mix_consts = [f32(norm_mix).reshape(DEPTH, 1, D_MODEL), w_in.astype(_BF16), vec512,
                  _pack_gate(lru_wa), _pack_gate(lru_wx), a8, w1, w2, w3, s5_w_glu.astype(_BF16)]
    gains = jnp.stack([f32(norm_ffn), jnp.broadcast_to(f32(norm_final), (DEPTH, D_MODEL))], axis=1)
    ffn_conv = jnp.concatenate([f32(ffn_conv_w), f32(ffn_conv_b)[:, None, :]], axis=1)
    ffn_consts = [w_out.astype(_BF16), gains, w_up.astype(_BF16), ffn_conv, w_down.astype(_BF16)]
    gn_gain = f32(ret_norm).reshape(DEPTH, 1, RET_WIDTH)
    h = x
    for l in range(DEPTH):
        first, last = l == 0, l == DEPTH - 1
        ya, qkvg = _mix_call(h, mix_consts, l, batch_major_in=first)
        yr = _ret_call(qkvg, cos2, sin2, gn_gain, l)
        h = _ffn_call(h, ya, yr, ffn_consts, l, batch_major_in=first, last=last)
    return h
```

```python
import functools
import math

import jax
import jax.numpy as jnp
from jax import lax
from jax.experimental import pallas as pl
from jax.experimental.pallas import tpu as pltpu

D_MODEL = 1024
BATCH = 8
SEQ = 2048
DEPTH = 2
ROWS = BATCH * SEQ

LRU_WIDTH = 512
LRU_BLOCKS = 8
LRU_CONV = 4
LRU_C = 8.0
S5_WIDTH = 512
S5_GROUP = 16
S5_GROUPS = 32
S5_STATE = 64
S5_STATES = S5_GROUPS * S5_STATE
RET_HEADS = 4
RET_HEAD_DIM = 128
RET_WIDTH = 512
ROPE_BASE = 10000.0
MIX_A_WIDTH = LRU_WIDTH + S5_WIDTH
PROJ_A_WIDTH = 2 * LRU_WIDTH + S5_WIDTH
QKVG_WIDTH = 4 * RET_WIDTH
IN_WIDTH = PROJ_A_WIDTH + QKVG_WIDTH
D_FF = 3 * D_MODEL
FFN_CONV = 3
NORM_EPS = 1e-6

SUBLANES = 8
LANES = 128
MXU_WIDTH = 256
VMEM_LIMIT_BYTES = 56 * 1024 * 1024

MIX_TILE_STEPS = 64
MIX_TILE_ROWS = MIX_TILE_STEPS * BATCH
FFN_TILE_STEPS = 64
FFN_TILE_ROWS = FFN_TILE_STEPS * BATCH
FFN_CHUNK = 1024
RET_CHUNK = 256
S5_SCAN_LANES = 512
S5_BLOCK = 8

_F32 = jnp.float32
_BF16 = jnp.bfloat16

V_CONV_W, V_CONV_B, V_BA, V_BX, V_SOFTPLUS, V_LRU_NORM, V_S5_D, V_B_GLU, V_S5_NORM = 0, 4, 5, 6, 7, 8, 9, 10, 11
V_ROWS = 16


def _gelu(x):
    return 0.5 * x * (1.0 + jnp.tanh(0.7978845608028654 * (x + 0.044715 * (x * x * x))))


def _sigmoid(x):
    return 1.0 / (1.0 + jnp.exp(-x))


def _rmsnorm(x, gain):
    return x * lax.rsqrt(jnp.mean(x * x, axis=-1, keepdims=True) + NORM_EPS) * gain


def _dot(a, b):
    return jnp.dot(a, b, preferred_element_type=_F32)


def _to_time_major(x_ref, slab, steps):
    n = x_ref.shape[-1] // LANES
    for b in range(BATCH):
        for s in range(n):
            slab[s, pl.ds(b, steps, stride=BATCH), :] = x_ref[b, :, s * LANES:(s + 1) * LANES]
    return jnp.concatenate([slab[s] for s in range(n)], axis=-1)


def _from_time_major(slab, o_ref, steps):
    n = o_ref.shape[-1] // LANES
    for b in range(BATCH):
        for s in range(n):
            o_ref[b, :, s * LANES:(s + 1) * LANES] = (
                slab[s, pl.ds(b, steps, stride=BATCH), :].astype(o_ref.dtype))


def _causal_conv(x, hist, taps, bias):
    tm = x.shape[0]
    y = bias
    for k, w in enumerate(taps):
        back = (len(taps) - 1 - k) * BATCH
        xs = x if back == 0 else jnp.concatenate([hist[hist.shape[0] - back:], x[:tm - back]], axis=0)
        y = y + w * xs
    return y


def _rotary_kernel(pos_ref, inv_ref, sign_ref, cos_ref, sin_ref):
    half = RET_HEAD_DIM // 2
    pairs = SEQ // 2
    low = lax.broadcasted_iota(jnp.int32, (pairs, RET_HEAD_DIM), 1) < half
    pos = jnp.where(low, pos_ref[0, :, 0:1], pos_ref[0, :, 1:2]).astype(_F32)
    ang = pos * inv_ref[...]
    for table, out_ref, scale in ((jnp.cos(ang), cos_ref, None), (jnp.sin(ang), sin_ref, sign_ref[...])):
        swapped = pltpu.roll(table, half, 1)
        even = jnp.where(low, table, swapped)
        odd = jnp.where(low, swapped, table)
        if scale is not None:
            even, odd = even * scale, odd * scale
        out_ref[0, pl.ds(0, pairs, stride=2), :] = even
        out_ref[0, pl.ds(1, pairs, stride=2), :] = odd


def _rotary_tables(positions):
    half = RET_HEAD_DIM // 2
    inv = ROPE_BASE ** (-jnp.arange(half, dtype=_F32) * 2.0 / RET_HEAD_DIM)
    inv2 = jnp.concatenate([inv, inv])[None, :]
    sign = jnp.concatenate([-jnp.ones((half,), _F32), jnp.ones((half,), _F32)])[None, :]
    pos3 = positions.reshape(BATCH, SEQ // 2, 2)
    out = jax.ShapeDtypeStruct((BATCH, SEQ, RET_HEAD_DIM), _F32)
    return pl.pallas_call(
        _rotary_kernel,
        grid=(BATCH,),
        in_specs=[pl.BlockSpec((1, SEQ // 2, 2), lambda b: (b, 0, 0)),
                  pl.BlockSpec((1, RET_HEAD_DIM), lambda b: (0, 0)),
                  pl.BlockSpec((1, RET_HEAD_DIM), lambda b: (0, 0))],
        out_specs=[pl.BlockSpec((1, SEQ, RET_HEAD_DIM), lambda b: (b, 0, 0)),
                   pl.BlockSpec((1, SEQ, RET_HEAD_DIM), lambda b: (b, 0, 0))],
        out_shape=[out, out],
        name="rotary_tables",
    )(pos3, inv2, sign)


def _mix_kernel(h_ref, gain_ref, w_in_ref, vec_ref, wa_ref, wx_ref,
                a8_ref, w1_ref, w2_ref, w3_ref, wglu_ref,
                ya_ref, qkvg_ref,
                conv_hist, gate_s, u_s, la_s, lb_s, lru_state, sr, si, y_s, s5_state_r, s5_state_i, qslab, hslab,
                *, batch_major_in):
    tm = MIX_TILE_ROWS
    vec = lambda row: vec_ref[row:row + 1, :]

    @pl.when(pl.program_id(0) == 0)
    def _init():
        conv_hist[...] = jnp.zeros_like(conv_hist)
        lru_state[...] = jnp.zeros_like(lru_state)
        s5_state_r[...] = jnp.zeros_like(s5_state_r)
        s5_state_i[...] = jnp.zeros_like(s5_state_i)

    h = _to_time_major(h_ref, hslab, MIX_TILE_STEPS) if batch_major_in else h_ref[...]
    xn = _rmsnorm(h, gain_ref[...]).astype(_BF16)

    lru_x = _dot(xn, w_in_ref[:, 0:LRU_WIDTH])
    u_s[...] = _dot(xn, w_in_ref[:, 2 * LRU_WIDTH:PROJ_A_WIDTH])

    xc = _causal_conv(lru_x, conv_hist[...], [vec(V_CONV_W + k) for k in range(LRU_CONV)], vec(V_CONV_B))
    conv_hist[...] = lru_x[tm - (LRU_CONV - 1) * BATCH:]
    xcb = xc.astype(_BF16)
    halves = range(LRU_WIDTH // MXU_WIDTH)
    pre_r = jnp.concatenate([_dot(xcb[:, MXU_WIDTH * p:MXU_WIDTH * (p + 1)], wa_ref[p]) for p in halves], axis=-1)
    pre_i = jnp.concatenate([_dot(xcb[:, MXU_WIDTH * p:MXU_WIDTH * (p + 1)], wx_ref[p]) for p in halves], axis=-1)

    nb = MIX_TILE_STEPS // S5_BLOCK
    gps = LANES // S5_GROUP
    lane_blk = lax.broadcasted_iota(jnp.int32, (nb * BATCH, LANES), 1) // S5_GROUP
    u = u_s[...]
    lag_rows = [jnp.concatenate([u[(S5_BLOCK * k + i) * BATCH:(S5_BLOCK * k + i + 1) * BATCH, :]
                                 for k in range(nb)], axis=0) for i in range(S5_BLOCK)]
    u_grp = []
    for sb in range(S5_WIDTH // LANES):
        rot = [lag_rows[i][:, sb * LANES:(sb + 1) * LANES] for i in range(S5_BLOCK)]
        rot = [r if i == 0 else pltpu.roll(r, i * S5_GROUP, 1) for i, r in enumerate(rot)]
        rot = [r.astype(_BF16) for r in rot]
        for q in range(gps):
            o = rot[S5_BLOCK - 1]
            for i in range(S5_BLOCK - 2, -1, -1):
                o = jnp.where(lane_blk == (q + i) % gps, rot[i], o)
            u_grp.append(o)
    n_pair = S5_GROUPS // 2
    u_pair = [jnp.concatenate([u_grp[2 * m], u_grp[2 * m + 1]], axis=1) for m in range(n_pair)]

    for m in range(n_pair):
        v = _dot(u_pair[m], w1_ref[m])
        sr[:, LANES * m:LANES * (m + 1)] = v[:, 0:LANES]
        si[:, LANES * m:LANES * (m + 1)] = v[:, LANES:2 * LANES]

    r = _sigmoid(pre_r + vec(V_BA))
    i = _sigmoid(pre_i + vec(V_BX))
    log_a = (-LRU_C) * r * vec(V_SOFTPLUS)
    a = jnp.exp(log_a)
    la_s[...] = a
    z = -jnp.tanh(log_a) * (1.0 + a * a)
    lb_s[...] = jnp.where(z == 0.0, 0.0, z * lax.rsqrt(z)) * (i * xc)

    for sg in range(S5_STATES // S5_SCAN_LANES):
        sl = slice(sg * S5_SCAN_LANES, (sg + 1) * S5_SCAN_LANES)
        ar = jnp.broadcast_to(a8_ref[0:1, sl], (BATCH, S5_SCAN_LANES))
        ai = jnp.broadcast_to(a8_ref[1:2, sl], (BATCH, S5_SCAN_LANES))
        xr = s5_state_r[:, sl]
        xi = s5_state_i[:, sl]
        for k in range(nb):
            rows = slice(k * BATCH, (k + 1) * BATCH)
            vr, vi = sr[rows, sl], si[rows, sl]
            sr[rows, sl] = xr
            si[rows, sl] = xi
            xr, xi = ar * xr - ai * xi + vr, ar * xi + ai * xr + vi
        s5_state_r[:, sl] = xr
        s5_state_i[:, sl] = xi

    def project_qkvg(j):
        res = _dot(xn, w_in_ref[:, PROJ_A_WIDTH + j * 512:PROJ_A_WIDTH + (j + 1) * 512])
        for s in range(512 // LANES):
            qslab[j * (512 // LANES) + s] = res[:, s * LANES:(s + 1) * LANES]

    gate_s[...] = _dot(xn, w_in_ref[:, LRU_WIDTH:2 * LRU_WIDTH])
    project_qkvg(0)
    project_qkvg(1)

    hs = lru_state[...]
    for t in range(MIX_TILE_STEPS):
        rows = slice(t * BATCH, (t + 1) * BATCH)
        hs = la_s[rows, :] * hs + lb_s[rows, :]
        lb_s[rows, :] = hs
    lru_state[...] = hs

    y_grp = []
    for m in range(n_pair):
        s_cat = jnp.concatenate([sr[:, LANES * m:LANES * (m + 1)], si[:, LANES * m:LANES * (m + 1)]],
                                axis=1).astype(_BF16)
        ym = (lax.dot_general(s_cat, w2_ref[m], (((1,), (1,)), ((), ())), preferred_element_type=_F32)
              + _dot(u_pair[m], w3_ref[m]))
        y_grp += [ym[:, 0:LANES], ym[:, LANES:2 * LANES]]
    for sb in range(S5_WIDTH // LANES):
        for j in range(S5_BLOCK):
            t = y_grp[sb * gps + gps - 1]
            for q in range(gps - 2, -1, -1):
                t = jnp.where(lane_blk == (q + j) % gps, y_grp[sb * gps + q], t)
            if j:
                t = pltpu.roll(t, LANES - j * S5_GROUP, 1)
            for k in range(nb):
                row = (S5_BLOCK * k + j) * BATCH
                y_s[row:row + BATCH, sb * LANES:(sb + 1) * LANES] = t[k * BATCH:(k + 1) * BATCH, :]
    y = y_s[...] + vec(V_S5_D) * u_s[...]
    z = _gelu(y)
    glu = _dot(z.astype(_BF16), wglu_ref[...])

    y_lru = lb_s[...] * _gelu(gate_s[...])
    ya_ref[:, 0:LRU_WIDTH] = _rmsnorm(y_lru, vec(V_LRU_NORM)).astype(ya_ref.dtype)
    out = z * _sigmoid(glu + vec(V_B_GLU))
    ya_ref[:, LRU_WIDTH:MIX_A_WIDTH] = _rmsnorm(out, vec(V_S5_NORM)).astype(ya_ref.dtype)
    project_qkvg(2)
    project_qkvg(3)
    _from_time_major(qslab, qkvg_ref, MIX_TILE_STEPS)


def _layer_spec(arr, l):
    nd = arr.ndim
    return pl.BlockSpec((None,) + arr.shape[1:], lambda *_: (l,) + (0,) * (nd - 1),
                        pipeline_mode=pl.Buffered(1))


def _stream_spec(batch_major, steps, width):
    if batch_major:
        return pl.BlockSpec((BATCH, steps, width), lambda i: (0, i, 0))
    return pl.BlockSpec((steps * BATCH, width), lambda i: (i, 0))


def _mix_call(h, consts, l, batch_major_in):
    tm = MIX_TILE_ROWS
    return pl.pallas_call(
        functools.partial(_mix_kernel, batch_major_in=batch_major_in),
        grid=(ROWS // tm,),
        in_specs=[_stream_spec(batch_major_in, MIX_TILE_STEPS, D_MODEL)] + [_layer_spec(c, l) for c in consts],
        out_specs=[pl.BlockSpec((tm, MIX_A_WIDTH), lambda i: (i, 0)),
                   _stream_spec(True, MIX_TILE_STEPS, QKVG_WIDTH)],
        out_shape=[jax.ShapeDtypeStruct((ROWS, MIX_A_WIDTH), _BF16),
                   jax.ShapeDtypeStruct((BATCH, SEQ, QKVG_WIDTH), _BF16)],
        scratch_shapes=[pltpu.VMEM(((LRU_CONV - 1) * BATCH, LRU_WIDTH), _F32),
                        pltpu.VMEM((tm, LRU_WIDTH), _F32),
                        pltpu.VMEM((tm, S5_WIDTH), _F32),
                        pltpu.VMEM((tm, LRU_WIDTH), _F32),
                        pltpu.VMEM((tm, LRU_WIDTH), _F32),
                        pltpu.VMEM((BATCH, LRU_WIDTH), _F32),
                        pltpu.VMEM((tm // S5_BLOCK, S5_STATES), _F32),
                        pltpu.VMEM((tm // S5_BLOCK, S5_STATES), _F32),
                        pltpu.VMEM((tm, S5_WIDTH), _F32),
                        pltpu.VMEM((BATCH, S5_STATES), _F32),
                        pltpu.VMEM((BATCH, S5_STATES), _F32),
                        pltpu.VMEM((QKVG_WIDTH // LANES, tm, LANES), _F32),
                        pltpu.VMEM((D_MODEL // LANES, tm, LANES), _F32)],
        compiler_params=pltpu.CompilerParams(dimension_semantics=("arbitrary",),
                                             vmem_limit_bytes=VMEM_LIMIT_BYTES),
        name="mix_lru_s5",
    )(h, *consts)


def _ret_kernel(qkvg_ref, cos_ref, sin_ref, gn_ref, o_ref,
                state_ref, decay_ref, qdec_ref, kdec_ref):
    C = RET_CHUNK
    Dh = RET_HEAD_DIM
    log_gammas = [math.log1p(-(2.0 ** (-5.0 - h))) for h in range(RET_HEADS)]

    @pl.when((pl.program_id(0) == 0) & (pl.program_id(1) == 0))
    def _tables():
        row = lax.broadcasted_iota(jnp.int32, (C, C), 0)
        col = lax.broadcasted_iota(jnp.int32, (C, C), 1)
        rel = (row - col).astype(_F32)
        idx = lax.broadcasted_iota(jnp.int32, (C, Dh), 0).astype(_F32)
        for h in range(RET_HEADS):
            lg = log_gammas[h]
            decay_ref[h] = jnp.where(rel >= 0.0, jnp.exp(lg * jnp.maximum(rel, 0.0)), 0.0) * (Dh ** -0.5)
            qdec_ref[h] = jnp.exp(lg * (idx + 1.0))
            kdec_ref[h] = jnp.exp(lg * (C - 1.0 - idx)) * (Dh ** -0.5)

    @pl.when(pl.program_id(1) == 0)
    def _init():
        state_ref[...] = jnp.zeros_like(state_ref)

    cos2 = cos_ref[0]
    sin2 = sin_ref[0]

    def rot(t):
        return t * cos2 + pltpu.roll(t, Dh // 2, 1) * sin2

    for h in range(RET_HEADS):
        hs = slice(h * Dh, (h + 1) * Dh)
        qh = rot(qkvg_ref[:, h * Dh:(h + 1) * Dh].astype(_F32))
        kh = rot(qkvg_ref[:, RET_WIDTH + h * Dh:RET_WIDTH + (h + 1) * Dh].astype(_F32))
        vb = qkvg_ref[:, 2 * RET_WIDTH + h * Dh:2 * RET_WIDTH + (h + 1) * Dh]
        scores = lax.dot_general(qh.astype(_BF16), kh.astype(_BF16), (((1,), (1,)), ((), ())),
                                 preferred_element_type=_F32) * decay_ref[h]
        intra = _dot(scores.astype(_BF16), vb)
        state = state_ref[h]
        cross = _dot((qh * qdec_ref[h]).astype(_BF16), state.astype(_BF16))
        kv = lax.dot_general((kh * kdec_ref[h]).astype(_BF16), vb, (((0,), (0,)), ((), ())),
                             preferred_element_type=_F32)
        state_ref[h] = math.exp(log_gammas[h] * C) * state + kv
        o = intra + cross
        mu = jnp.mean(o, axis=-1, keepdims=True)
        oc = o - mu
        var = jnp.mean(oc * oc, axis=-1, keepdims=True)
        on = oc * lax.rsqrt(var + NORM_EPS) * gn_ref[:, hs]
        g = qkvg_ref[:, 3 * RET_WIDTH + h * Dh:3 * RET_WIDTH + (h + 1) * Dh].astype(_F32)
        o_ref[:, hs] = (on * (g * _sigmoid(g))).astype(o_ref.dtype)


def _ret_call(qkvg, cos2, sin2, gn_gain, l):
    C = RET_CHUNK
    return pl.pallas_call(
        _ret_kernel,
        grid=(BATCH, SEQ // C),
        in_specs=[pl.BlockSpec((None, C, QKVG_WIDTH), lambda b, n: (b, n, 0)),
                  pl.BlockSpec((1, C, RET_HEAD_DIM), lambda b, n: (b, n, 0)),
                  pl.BlockSpec((1, C, RET_HEAD_DIM), lambda b, n: (b, n, 0)),
                  pl.BlockSpec((None, 1, RET_WIDTH), lambda b, n: (l, 0, 0))],
        out_specs=pl.BlockSpec((None, C, RET_WIDTH), lambda b, n: (b, n, 0)),
        out_shape=jax.ShapeDtypeStruct((BATCH, SEQ, RET_WIDTH), _F32),
        scratch_shapes=[pltpu.VMEM((RET_HEADS, RET_HEAD_DIM, RET_HEAD_DIM), _F32),
                        pltpu.VMEM((RET_HEADS, C, C), _F32),
                        pltpu.VMEM((RET_HEADS, C, RET_HEAD_DIM), _F32),
                        pltpu.VMEM((RET_HEADS, C, RET_HEAD_DIM), _F32)],
        compiler_params=pltpu.CompilerParams(dimension_semantics=("arbitrary", "arbitrary"),
                                             vmem_limit_bytes=VMEM_LIMIT_BYTES),
        name="retention",
    )(qkvg, cos2, sin2, gn_gain)


def _ffn_kernel(h_ref, ya_ref, yr_ref, wo_ref, gains_ref, w_up_ref, conv_ref,
                w_down_ref, o_ref, acc, carry, yslab, hslab, *, batch_major_in, last):
    tm = FFN_TILE_ROWS
    fc = FFN_CHUNK

    @pl.when(pl.program_id(0) == 0)
    def _init():
        carry[...] = jnp.zeros_like(carry)

    h = _to_time_major(h_ref, hslab, FFN_TILE_STEPS) if batch_major_in else h_ref[...]
    yr = _to_time_major(yr_ref, yslab, FFN_TILE_STEPS).astype(_BF16)
    h1 = _dot(ya_ref[...], wo_ref[0:MIX_A_WIDTH, :]) + _dot(yr, wo_ref[MIX_A_WIDTH:, :]) + h
    acc[...] = h1
    xn = _rmsnorm(h1, gains_ref[0:1, :]).astype(_BF16)

    def up_proj(j):
        return (_dot(xn, w_up_ref[:, j * fc:(j + 1) * fc]),
                _dot(xn, w_up_ref[:, D_FF + j * fc:D_FF + (j + 1) * fc]))

    n_chunks = D_FF // fc
    ups = up_proj(0)
    for j in range(n_chunks):
        nxt = up_proj(j + 1) if j + 1 < n_chunks else None
        vs = slice(j * fc, (j + 1) * fc)
        gs = slice(D_FF + j * fc, D_FF + (j + 1) * fc)
        upv, upg = ups
        cv = _causal_conv(upv, carry[:, vs], [conv_ref[k:k + 1, vs] for k in range(FFN_CONV)],
                          conv_ref[FFN_CONV:FFN_CONV + 1, vs])
        cg = _causal_conv(upg, carry[:, gs], [conv_ref[k:k + 1, gs] for k in range(FFN_CONV)],
                          conv_ref[FFN_CONV:FFN_CONV + 1, gs])
        carry[:, vs] = upv[tm - (FFN_CONV - 1) * BATCH:]
        carry[:, gs] = upg[tm - (FFN_CONV - 1) * BATCH:]
        act = (_gelu(cg) * cv).astype(_BF16)
        down = _dot(act, w_down_ref[vs, :])
        if j + 1 < n_chunks:
            acc[...] += down
        ups = nxt
    total = acc[...] + down
    if last:
        out = _rmsnorm(total, gains_ref[1:2, :])
        for s in range(D_MODEL // LANES):
            hslab[s] = out[:, s * LANES:(s + 1) * LANES]
        _from_time_major(hslab, o_ref, FFN_TILE_STEPS)
    else:
        o_ref[...] = total


def _ffn_call(h, ya, yr, consts, l, batch_major_in, last):
    tm = FFN_TILE_ROWS
    out_shape = (BATCH, SEQ, D_MODEL) if last else (ROWS, D_MODEL)
    return pl.pallas_call(
        functools.partial(_ffn_kernel, batch_major_in=batch_major_in, last=last),
        grid=(ROWS // tm,),
        in_specs=[_stream_spec(batch_major_in, FFN_TILE_STEPS, D_MODEL),
                  pl.BlockSpec((tm, MIX_A_WIDTH), lambda i: (i, 0)),
                  _stream_spec(True, FFN_TILE_STEPS, RET_WIDTH)]
                 + [_layer_spec(c, l) for c in consts],
        out_specs=_stream_spec(last, FFN_TILE_STEPS, D_MODEL),
        out_shape=jax.ShapeDtypeStruct(out_shape, _F32),
        scratch_shapes=[pltpu.VMEM((tm, D_MODEL), _F32),
                        pltpu.VMEM(((FFN_CONV - 1) * BATCH, 2 * D_FF), _F32),
                        pltpu.VMEM((RET_WIDTH // LANES, tm, LANES), _F32),
                        pltpu.VMEM((D_MODEL // LANES, tm, LANES), _F32)],
        compiler_params=pltpu.CompilerParams(dimension_semantics=("arbitrary",),
                                             vmem_limit_bytes=VMEM_LIMIT_BYTES),
        name="outproj_ffn",
    )(h, ya, yr, *consts)


def _pack_gate(w):
    blk = LRU_WIDTH // LRU_BLOCKS
    per = MXU_WIDTH // blk
    w5 = w.astype(_F32).reshape(DEPTH, LRU_BLOCKS // per, per, blk, blk)
    t = jnp.einsum('lphij,hg->lphigj', w5, jnp.eye(per, dtype=_F32))
    return t.reshape(DEPTH, LRU_BLOCKS // per, MXU_WIDTH, MXU_WIDTH).astype(_BF16)


def _s5_params(lam_re, lam_im, log_dt, b_re, b_im, c_re, c_im):
    T = S5_BLOCK
    dt = jnp.exp(log_dt.astype(_F32))[..., None]
    lr, li = lam_re.astype(_F32), lam_im.astype(_F32)
    mag = jnp.exp(lr * dt)
    abar_re, abar_im = mag * jnp.cos(li * dt), mag * jnp.sin(li * dt)
    den = lr * lr + li * li
    nr, ni = abar_re - 1.0, abar_im
    coef_re = (nr * lr + ni * li) / den
    coef_im = (ni * lr - nr * li) / den
    bt_re = jnp.swapaxes(b_re.astype(_F32), -1, -2)
    bt_im = jnp.swapaxes(b_im.astype(_F32), -1, -2)
    bbar_re = coef_re[:, :, None, :] * bt_re - coef_im[:, :, None, :] * bt_im
    bbar_im = coef_re[:, :, None, :] * bt_im + coef_im[:, :, None, :] * bt_re
    cr, ci = c_re.astype(_F32), c_im.astype(_F32)

    def apow(e):
        e = e.astype(_F32)
        m = jnp.exp(lr[:, :, None, :] * dt[:, :, None, :] * e)
        ang = li[:, :, None, :] * dt[:, :, None, :] * e
        return m * jnp.cos(ang), m * jnp.sin(ang)

    gidx = jnp.arange(S5_GROUPS)
    lag_of = (jnp.arange(T)[None, :] - (gidx % (LANES // S5_GROUP))[:, None]) % T
    even = (gidx % 2 == 0).astype(_F32)[None, :, None, None, None]
    odd = 1.0 - even

    def pair_rows(x):
        return x.reshape(DEPTH, S5_GROUPS // 2, 2 * T * S5_GROUP, x.shape[-1])

    def spread(re, im):
        return jnp.concatenate([re * even, re * odd, im * even, im * odd], axis=-1)

    p_re, p_im = apow((T - 1 - lag_of)[None, :, :, None])
    p_re, p_im = p_re[:, :, :, None, :], p_im[:, :, :, None, :]
    w1 = pair_rows(spread(p_re * bbar_re[:, :, None] - p_im * bbar_im[:, :, None],
                          p_re * bbar_im[:, :, None] + p_im * bbar_re[:, :, None]))
    p_re, p_im = apow((lag_of + 1)[None, :, :, None])
    p_re, p_im = p_re[:, :, :, None, :], p_im[:, :, :, None, :]
    w2t = pair_rows(spread(cr[:, :, None] * p_re - ci[:, :, None] * p_im,
                           -(cr[:, :, None] * p_im + ci[:, :, None] * p_re)))
    p_re, p_im = apow(jnp.arange(T)[None, None, :, None])
    x_re = p_re[:, :, :, None, :] * bbar_re[:, :, None] - p_im[:, :, :, None, :] * bbar_im[:, :, None]
    x_im = p_re[:, :, :, None, :] * bbar_im[:, :, None] + p_im[:, :, :, None, :] * bbar_re[:, :, None]
    kt = jnp.einsum('lgtcp,lgop->lgtco', x_re, cr) - jnp.einsum('lgtcp,lgop->lgtco', x_im, ci)
    kexp = jnp.broadcast_to(kt[:, :, :, :, None, :], (DEPTH, S5_GROUPS, T, S5_GROUP, T, S5_GROUP))
    kexp = kexp.reshape(DEPTH, S5_GROUPS, T, S5_GROUP, T * S5_GROUP)
    tau = lag_of[:, None, :] - lag_of[:, :, None]
    w3g = 0.0
    for t in range(T):
        sel = jnp.repeat((tau == t).astype(_F32), S5_GROUP, axis=-1)
        w3g = w3g + sel[None, :, :, None, :] * kexp[:, :, t][:, :, None, :, :]
    w3 = pair_rows(jnp.concatenate([w3g * even, w3g * odd], axis=-1))

    a_re, a_im = apow(jnp.full((1, 1, 1, 1), T))
    a8 = jnp.concatenate([a_re.reshape(DEPTH, 1, S5_STATES), a_im.reshape(DEPTH, 1, S5_STATES)], axis=1)
    return a8, w1.astype(_BF16), w2t.astype(_BF16), w3.astype(_BF16)


def kernel(x, positions, norm_mix, w_in, lru_conv_w, lru_conv_b, lru_wa, lru_ba, lru_wx, lru_bx, lru_lambda, lru_norm, s5_lambda_re, s5_lambda_im, s5_log_dt, s5_b_re, s5_b_im, s5_c_re, s5_c_im, s5_d, s5_w_glu, s5_b_glu, s5_norm, ret_norm, w_out, norm_ffn, w_up, ffn_conv_w, ffn_conv_b, w_down, norm_final):
    cos2, sin2 = _rotary_tables(positions)
    f32 = lambda v: v.astype(_F32)
    vec512 = jnp.concatenate(
        [f32(lru_conv_w),
         jnp.stack([f32(lru_conv_b), f32(lru_ba).reshape(DEPTH, LRU_WIDTH), f32(lru_bx).reshape(DEPTH, LRU_WIDTH),
                    jax.nn.softplus(-f32(lru_lambda)), f32(lru_norm), f32(s5_d), f32(s5_b_glu), f32(s5_norm)], axis=1),
         jnp.zeros((DEPTH, V_ROWS - V_S5_NORM - 1, LRU_WIDTH), _F32)], axis=1)
    a8, w1, w2, w3 = _s5_params(s5_lambda_re, s5_lambda_im, s5_log_dt, s5_b_re, s5_b_im, s5_c_re, s5_c_im)
    mix_consts = [f32(norm_mix).reshape(DEPTH, 1, D_MODEL), w_in.astype(_BF16), vec512,
                  _pack_gate(lru_wa), _pack_gate(lru_wx), a8, w1, w2, w3, s5_w_glu.astype(_BF16)]
    gains = jnp.stack([f32(norm_ffn), jnp.broadcast_to(f32(norm_final), (DEPTH, D_MODEL))], axis=1)
    ffn_conv = jnp.concatenate([f32(ffn_conv_w), f32(ffn_conv_b)[:, None, :]], axis=1)
    ffn_consts = [w_out.astype(_BF16), gains, w_up.astype(_BF16), ffn_conv, w_down.astype(_BF16)]
    gn_gain = f32(ret_norm).reshape(DEPTH, 1, RET_WIDTH)
    h = x
    for l in range(DEPTH):
        first, last = l == 0, l == DEPTH - 1
        ya, qkvg = _mix_call(h, mix_consts, l, batch_major_in=first)
        yr = _ret_call(qkvg, cos2, sin2, gn_gain, l)
        h = _ffn_call(h, ya, yr, ffn_consts, l, batch_major_in=first, last=last)
    return h
```

```python
import functools
import math

import jax
import jax.numpy as jnp
from jax import lax
from jax.experimental import pallas as pl
from jax.experimental.pallas import tpu as pltpu

D_MODEL = 1024
BATCH = 8
SEQ = 2048
DEPTH = 2
ROWS = BATCH * SEQ

LRU_WIDTH = 512
LRU_BLOCKS = 8
LRU_CONV = 4
LRU_C = 8.0
S5_WIDTH = 512
S5_GROUP = 16
S5_GROUPS = 32
S5_STATE = 64
S5_STATES = S5_GROUPS * S5_STATE
RET_HEADS = 4
RET_HEAD_DIM = 128
RET_WIDTH = 512
ROPE_BASE = 10000.0
MIX_A_WIDTH = LRU_WIDTH + S5_WIDTH
PROJ_A_WIDTH = 2 * LRU_WIDTH + S5_WIDTH
QKVG_WIDTH = 4 * RET_WIDTH
IN_WIDTH = PROJ_A_WIDTH + QKVG_WIDTH
D_FF = 3 * D_MODEL
FFN_CONV = 3
NORM_EPS = 1e-6

SUBLANES = 8
LANES = 128
MXU_WIDTH = 256
VMEM_LIMIT_BYTES = 56 * 1024 * 1024

MIX_TILE_STEPS = 64
MIX_TILE_ROWS = MIX_TILE_STEPS * BATCH
FFN_TILE_STEPS = 64
FFN_TILE_ROWS = FFN_TILE_STEPS * BATCH
FFN_CHUNK = 1024
RET_CHUNK = 256
S5_SCAN_LANES = 512
S5_BLOCK = 8

_F32 = jnp.float32
_BF16 = jnp.bfloat16

V_CONV_W, V_CONV_B, V_BA, V_BX, V_SOFTPLUS, V_LRU_NORM, V_S5_D, V_B_GLU, V_S5_NORM = 0, 4, 5, 6, 7, 8, 9, 10, 11
V_ROWS = 16


def _gelu(x):
    return 0.5 * x * (1.0 + jnp.tanh(0.7978845608028654 * (x + 0.044715 * (x * x * x))))


def _sigmoid(x):
    return 1.0 / (1.0 + jnp.exp(-x))


def _rmsnorm(x, gain):
    return x * lax.rsqrt(jnp.mean(x * x, axis=-1, keepdims=True) + NORM_EPS) * gain


def _dot(a, b):
    return jnp.dot(a, b, preferred_element_type=_F32)


def _to_time_major(x_ref, slab, steps):
    n = x_ref.shape[-1] // LANES
    for b in range(BATCH):
        for s in range(n):
            slab[s, pl.ds(b, steps, stride=BATCH), :] = x_ref[b, :, s * LANES:(s + 1) * LANES]
    return jnp.concatenate([slab[s] for s in range(n)], axis=-1)


def _from_time_major(slab, o_ref, steps):
    n = o_ref.shape[-1] // LANES
    for b in range(BATCH):
        for s in range(n):
            o_ref[b, :, s * LANES:(s + 1) * LANES] = (
                slab[s, pl.ds(b, steps, stride=BATCH), :].astype(o_ref.dtype))


def _causal_conv(x, hist, taps, bias):
    tm = x.shape[0]
    y = bias
    for k, w in enumerate(taps):
        back = (len(taps) - 1 - k) * BATCH
        xs = x if back == 0 else jnp.concatenate([hist[hist.shape[0] - back:], x[:tm - back]], axis=0)
        y = y + w * xs
    return y


def _rotary_kernel(pos_ref, inv_ref, sign_ref, cos_ref, sin_ref):
    half = RET_HEAD_DIM // 2
    pairs = SEQ // 2
    low = lax.broadcasted_iota(jnp.int32, (pairs, RET_HEAD_DIM), 1) < half
    pos = jnp.where(low, pos_ref[0, :, 0:1], pos_ref[0, :, 1:2]).astype(_F32)
    ang = pos * inv_ref[...]
    for table, out_ref, scale in ((jnp.cos(ang), cos_ref, None), (jnp.sin(ang), sin_ref, sign_ref[...])):
        swapped = pltpu.roll(table, half, 1)
        even = jnp.where(low, table, swapped)
        odd = jnp.where(low, swapped, table)
        if scale is not None:
            even, odd = even * scale, odd * scale
        out_ref[0, pl.ds(0, pairs, stride=2), :] = even
        out_ref[0, pl.ds(1, pairs, stride=2), :] = odd


def _rotary_tables(positions):
    half = RET_HEAD_DIM // 2
    inv = ROPE_BASE ** (-jnp.arange(half, dtype=_F32) * 2.0 / RET_HEAD_DIM)
    inv2 = jnp.concatenate([inv, inv])[None, :]
    sign = jnp.concatenate([-jnp.ones((half,), _F32), jnp.ones((half,), _F32)])[None, :]
    pos3 = positions.reshape(BATCH, SEQ // 2, 2)
    out = jax.ShapeDtypeStruct((BATCH, SEQ, RET_HEAD_DIM), _F32)
    return pl.pallas_call(
        _rotary_kernel,
        grid=(BATCH,),
        in_specs=[pl.BlockSpec((1, SEQ // 2, 2), lambda b: (b, 0, 0)),
                  pl.BlockSpec((1, RET_HEAD_DIM), lambda b: (0, 0)),
                  pl.BlockSpec((1, RET_HEAD_DIM), lambda b: (0, 0))],
        out_specs=[pl.BlockSpec((1, SEQ, RET_HEAD_DIM), lambda b: (b, 0, 0)),
                   pl.BlockSpec((1, SEQ, RET_HEAD_DIM), lambda b: (b, 0, 0))],
        out_shape=[out, out],
        name="rotary_tables",
    )(pos3, inv2, sign)


def _mix_kernel(h_ref, gain_ref, w_in_ref, vec_ref, wa_ref, wx_ref,
                a8_ref, w1_ref, w2_ref, w3_ref, wglu_ref,
                ya_ref, qkvg_ref,
                conv_hist, gate_s, u_s, la_s, lb_s, lru_state, sr, si, y_s, s5_state_r, s5_state_i, qslab, hslab,
                *, batch_major_in):
    tm = MIX_TILE_ROWS
    vec = lambda row: vec_ref[row:row + 1, :]

    @pl.when(pl.program_id(0) == 0)
    def _init():
        conv_hist[...] = jnp.zeros_like(conv_hist)
        lru_state[...] = jnp.zeros_like(lru_state)
        s5_state_r[...] = jnp.zeros_like(s5_state_r)
        s5_state_i[...] = jnp.zeros_like(s5_state_i)

    h = _to_time_major(h_ref, hslab, MIX_TILE_STEPS) if batch_major_in else h_ref[...]
    xn = _rmsnorm(h, gain_ref[...]).astype(_BF16)

    lru_x = _dot(xn, w_in_ref[:, 0:LRU_WIDTH])
    u_s[...] = _dot(xn, w_in_ref[:, 2 * LRU_WIDTH:PROJ_A_WIDTH])

    xc = _causal_conv(lru_x, conv_hist[...], [vec(V_CONV_W + k) for k in range(LRU_CONV)], vec(V_CONV_B))
    conv_hist[...] = lru_x[tm - (LRU_CONV - 1) * BATCH:]
    xcb = xc.astype(_BF16)
    halves = range(LRU_WIDTH // MXU_WIDTH)
    pre_r = jnp.concatenate([_dot(xcb[:, MXU_WIDTH * p:MXU_WIDTH * (p + 1)], wa_ref[p]) for p in halves], axis=-1)
    pre_i = jnp.concatenate([_dot(xcb[:, MXU_WIDTH * p:MXU_WIDTH * (p + 1)], wx_ref[p]) for p in halves], axis=-1)

    nb = MIX_TILE_STEPS // S5_BLOCK
    gps = LANES // S5_GROUP
    lane_blk = lax.broadcasted_iota(jnp.int32, (nb * BATCH, LANES), 1) // S5_GROUP
    u = u_s[...]
    lag_rows = [jnp.concatenate([u[(S5_BLOCK * k + i) * BATCH:(S5_BLOCK * k + i + 1) * BATCH, :]
                                 for k in range(nb)], axis=0) for i in range(S5_BLOCK)]
    u_grp = []
    for sb in range(S5_WIDTH // LANES):
        rot = [lag_rows[i][:, sb * LANES:(sb + 1) * LANES] for i in range(S5_BLOCK)]
        rot = [r if i == 0 else pltpu.roll(r, i * S5_GROUP, 1) for i, r in enumerate(rot)]
        rot = [r.astype(_BF16) for r in rot]
        for q in range(gps):
            o = rot[S5_BLOCK - 1]
            for i in range(S5_BLOCK - 2, -1, -1):
                o = jnp.where(lane_blk == (q + i) % gps, rot[i], o)
            u_grp.append(o)
    n_pair = S5_GROUPS // 2
    u_pair = [jnp.concatenate([u_grp[2 * m], u_grp[2 * m + 1]], axis=1) for m in range(n_pair)]

    for m in range(n_pair):
        v = _dot(u_pair[m], w1_ref[m])
        sr[:, LANES * m:LANES * (m + 1)] = v[:, 0:LANES]
        si[:, LANES * m:LANES * (m + 1)] = v[:, LANES:2 * LANES]

    r = _sigmoid(pre_r + vec(V_BA))
    i = _sigmoid(pre_i + vec(V_BX))
    log_a = (-LRU_C) * r * vec(V_SOFTPLUS)
    a = jnp.exp(log_a)
    la_s[...] = a
    z = -jnp.tanh(log_a) * (1.0 + a * a)
    lb_s[...] = jnp.where(z == 0.0, 0.0, z * lax.rsqrt(z)) * (i * xc)

    for sg in range(S5_STATES // S5_SCAN_LANES):
        sl = slice(sg * S5_SCAN_LANES, (sg + 1) * S5_SCAN_LANES)
        ar = jnp.broadcast_to(a8_ref[0:1, sl], (BATCH, S5_SCAN_LANES))
        ai = jnp.broadcast_to(a8_ref[1:2, sl], (BATCH, S5_SCAN_LANES))
        xr = s5_state_r[:, sl]
        xi = s5_state_i[:, sl]
        for k in range(nb):
            rows = slice(k * BATCH, (k + 1) * BATCH)
            vr, vi = sr[rows, sl], si[rows, sl]
            sr[rows, sl] = xr
            si[rows, sl] = xi
            xr, xi = ar * xr - ai * xi + vr, ar * xi + ai * xr + vi
        s5_state_r[:, sl] = xr
        s5_state_i[:, sl] = xi

    def project_qkvg(j):
        res = _dot(xn, w_in_ref[:, PROJ_A_WIDTH + j * 512:PROJ_A_WIDTH + (j + 1) * 512])
        for s in range(512 // LANES):
            qslab[j * (512 // LANES) + s] = res[:, s * LANES:(s + 1) * LANES]

    gate_s[...] = _dot(xn, w_in_ref[:, LRU_WIDTH:2 * LRU_WIDTH])
    project_qkvg(0)
    project_qkvg(1)

    hs = lru_state[...]
    for t in range(MIX_TILE_STEPS):
        rows = slice(t * BATCH, (t + 1) * BATCH)
        hs = la_s[rows, :] * hs + lb_s[rows, :]
        lb_s[rows, :] = hs
    lru_state[...] = hs

    y_grp = []
    for m in range(n_pair):
        s_cat = jnp.concatenate([sr[:, LANES * m:LANES * (m + 1)], si[:, LANES * m:LANES * (m + 1)]],
                                axis=1).astype(_BF16)
        ym = (lax.dot_general(s_cat, w2_ref[m], (((1,), (1,)), ((), ())), preferred_element_type=_F32)
              + _dot(u_pair[m], w3_ref[m]))
        y_grp += [ym[:, 0:LANES], ym[:, LANES:2 * LANES]]
    for sb in range(S5_WIDTH // LANES):
        for j in range(S5_BLOCK):
            t = y_grp[sb * gps + gps - 1]
            for q in range(gps - 2, -1, -1):
                t = jnp.where(lane_blk == (q + j) % gps, y_grp[sb * gps + q], t)
            if j:
                t = pltpu.roll(t, LANES - j * S5_GROUP, 1)
            for k in range(nb):
                row = (S5_BLOCK * k + j) * BATCH
                y_s[row:row + BATCH, sb * LANES:(sb + 1) * LANES] = t[k * BATCH:(k + 1) * BATCH, :]
    y = y_s[...] + vec(V_S5_D) * u_s[...]
    z = _gelu(y)
    glu = _dot(z.astype(_BF16), wglu_ref[...])

    y_lru = lb_s[...] * _gelu(gate_s[...])
    ya_ref[:, 0:LRU_WIDTH] = _rmsnorm(y_lru, vec(V_LRU_NORM)).astype(ya_ref.dtype)
    out = z * _sigmoid(glu + vec(V_B_GLU))
    ya_ref[:, LRU_WIDTH:MIX_A_WIDTH] = _rmsnorm(out, vec(V_S5_NORM)).astype(ya_ref.dtype)
    project_qkvg(2)
    project_qkvg(3)
    _from_time_major(qslab, qkvg_ref, MIX_TILE_STEPS)


def _layer_spec(arr, l):
    nd = arr.ndim
    return pl.BlockSpec((None,) + arr.shape[1:], lambda *_: (l,) + (0,) * (nd - 1),
                        pipeline_mode=pl.Buffered(1))


def _stream_spec(batch_major, steps, width):
    if batch_major:
        return pl.BlockSpec((BATCH, steps, width), lambda i: (0, i, 0))
    return pl.BlockSpec((steps * BATCH, width), lambda i: (i, 0))


def _mix_call(h, consts, l, batch_major_in):
    tm = MIX_TILE_ROWS
    return pl.pallas_call(
        functools.partial(_mix_kernel, batch_major_in=batch_major_in),
        grid=(ROWS // tm,),
        in_specs=[_stream_spec(batch_major_in, MIX_TILE_STEPS, D_MODEL)] + [_layer_spec(c, l) for c in consts],
        out_specs=[pl.BlockSpec((tm, MIX_A_WIDTH), lambda i: (i, 0)),
                   _stream_spec(True, MIX_TILE_STEPS, QKVG_WIDTH)],
        out_shape=[jax.ShapeDtypeStruct((ROWS, MIX_A_WIDTH), _BF16),
                   jax.ShapeDtypeStruct((BATCH, SEQ, QKVG_WIDTH), _BF16)],
        scratch_shapes=[pltpu.VMEM(((LRU_CONV - 1) * BATCH, LRU_WIDTH), _F32),
                        pltpu.VMEM((tm, LRU_WIDTH), _F32),
                        pltpu.VMEM((tm, S5_WIDTH), _F32),
                        pltpu.VMEM((tm, LRU_WIDTH), _F32),
                        pltpu.VMEM((tm, LRU_WIDTH), _F32),
                        pltpu.VMEM((BATCH, LRU_WIDTH), _F32),
                        pltpu.VMEM((tm // S5_BLOCK, S5_STATES), _F32),
                        pltpu.VMEM((tm // S5_BLOCK, S5_STATES), _F32),
                        pltpu.VMEM((tm, S5_WIDTH), _F32),
                        pltpu.VMEM((BATCH, S5_STATES), _F32),
                        pltpu.VMEM((BATCH, S5_STATES), _F32),
                        pltpu.VMEM((QKVG_WIDTH // LANES, tm, LANES), _F32),
                        pltpu.VMEM((D_MODEL // LANES, tm, LANES), _F32)],
        compiler_params=pltpu.CompilerParams(dimension_semantics=("arbitrary",),
                                             vmem_limit_bytes=VMEM_LIMIT_BYTES),
        name="mix_lru_s5",
    )(h, *consts)


def _ret_kernel(qkvg_ref, cos_ref, sin_ref, gn_ref, o_ref,
                state_ref, decay_ref, qdec_ref, kdec_ref):
    C = RET_CHUNK
    Dh = RET_HEAD_DIM
    log_gammas = [math.log1p(-(2.0 ** (-5.0 - h))) for h in range(RET_HEADS)]

    @pl.when((pl.program_id(0) == 0) & (pl.program_id(1) == 0))
    def _tables():
        row = lax.broadcasted_iota(jnp.int32, (C, C), 0)
        col = lax.broadcasted_iota(jnp.int32, (C, C), 1)
        rel = (row - col).astype(_F32)
        idx = lax.broadcasted_iota(jnp.int32, (C, Dh), 0).astype(_F32)
        for h in range(RET_HEADS):
            lg = log_gammas[h]
            decay_ref[h] = jnp.where(rel >= 0.0, jnp.exp(lg * jnp.maximum(rel, 0.0)), 0.0) * (Dh ** -0.5)
            qdec_ref[h] = jnp.exp(lg * (idx + 1.0))
            kdec_ref[h] = jnp.exp(lg * (C - 1.0 - idx)) * (Dh ** -0.5)

    @pl.when(pl.program_id(1) == 0)
    def _init():
        state_ref[...] = jnp.zeros_like(state_ref)

    cos2 = cos_ref[0]
    sin2 = sin_ref[0]

    def rot(t):
        return t * cos2 + pltpu.roll(t, Dh // 2, 1) * sin2

    for h in range(RET_HEADS):
        hs = slice(h * Dh, (h + 1) * Dh)
        qh = rot(qkvg_ref[:, h * Dh:(h + 1) * Dh].astype(_F32))
        kh = rot(qkvg_ref[:, RET_WIDTH + h * Dh:RET_WIDTH + (h + 1) * Dh].astype(_F32))
        vb = qkvg_ref[:, 2 * RET_WIDTH + h * Dh:2 * RET_WIDTH + (h + 1) * Dh]
        scores = lax.dot_general(qh.astype(_BF16), kh.astype(_BF16), (((1,), (1,)), ((), ())),
                                 preferred_element_type=_F32) * decay_ref[h]
        intra = _dot(scores.astype(_BF16), vb)
        state = state_ref[h]
        cross = _dot((qh * qdec_ref[h]).astype(_BF16), state.astype(_BF16))
        kv = lax.dot_general((kh * kdec_ref[h]).astype(_BF16), vb, (((0,), (0,)), ((), ())),
                             preferred_element_type=_F32)
        state_ref[h] = math.exp(log_gammas[h] * C) * state + kv
        o = intra + cross
        mu = jnp.mean(o, axis=-1, keepdims=True)
        oc = o - mu
        var = jnp.mean(oc * oc, axis=-1, keepdims=True)
        on = oc * lax.rsqrt(var + NORM_EPS) * gn_ref[:, hs]
        g = qkvg_ref[:, 3 * RET_WIDTH + h * Dh:3 * RET_WIDTH + (h + 1) * Dh].astype(_F32)
        o_ref[:, hs] = (on * (g * _sigmoid(g))).astype(o_ref.dtype)


def _ret_call(qkvg, cos2, sin2, gn_gain, l):
    C = RET_CHUNK
    return pl.pallas_call(
        _ret_kernel,
        grid=(BATCH, SEQ // C),
        in_specs=[pl.BlockSpec((None, C, QKVG_WIDTH), lambda b, n: (b, n, 0)),
                  pl.BlockSpec((1, C, RET_HEAD_DIM), lambda b, n: (b, n, 0)),
                  pl.BlockSpec((1, C, RET_HEAD_DIM), lambda b, n: (b, n, 0)),
                  pl.BlockSpec((None, 1, RET_WIDTH), lambda b, n: (l, 0, 0))],
        out_specs=pl.BlockSpec((None, C, RET_WIDTH), lambda b, n: (b, n, 0)),
        out_shape=jax.ShapeDtypeStruct((BATCH, SEQ, RET_WIDTH), _F32),
        scratch_shapes=[pltpu.VMEM((RET_HEADS, RET_HEAD_DIM, RET_HEAD_DIM), _F32),
                        pltpu.VMEM((RET_HEADS, C, C), _F32),
                        pltpu.VMEM((RET_HEADS, C, RET_HEAD_DIM), _F32),
                        pltpu.VMEM((RET_HEADS, C, RET_HEAD_DIM), _F32)],
        compiler_params=pltpu.CompilerParams(dimension_semantics=("arbitrary", "arbitrary"),
                                             vmem_limit_bytes=VMEM_LIMIT_BYTES),
        name="retention",
    )(qkvg, cos2, sin2, gn_gain)


def _ffn_kernel(h_ref, ya_ref, yr_ref, wo_ref, gains_ref, w_up_ref, conv_ref,
                w_down_ref, o_ref, acc, carry, yslab, hslab, *, batch_major_in, last):
    tm = FFN_TILE_ROWS
    fc = FFN_CHUNK

    @pl.when(pl.program_id(0) == 0)
    def _init():
        carry[...] = jnp.zeros_like(carry)

    h = _to_time_major(h_ref, hslab, FFN_TILE_STEPS) if batch_major_in else h_ref[...]
    yr = _to_time_major(yr_ref, yslab, FFN_TILE_STEPS).astype(_BF16)
    h1 = _dot(ya_ref[...], wo_ref[0:MIX_A_WIDTH, :]) + _dot(yr, wo_ref[MIX_A_WIDTH:, :]) + h
    acc[...] = h1
    xn = _rmsnorm(h1, gains_ref[0:1, :]).astype(_BF16)

    def up_proj(j):
        return (_dot(xn, w_up_ref[:, j * fc:(j + 1) * fc]),
                _dot(xn, w_up_ref[:, D_FF + j * fc:D_FF + (j + 1) * fc]))

    n_chunks = D_FF // fc
    ups = up_proj(0)
    for j in range(n_chunks):
        nxt = up_proj(j + 1) if j + 1 < n_chunks else None
        vs = slice(j * fc, (j + 1) * fc)
        gs = slice(D_FF + j * fc, D_FF + (j + 1) * fc)
        upv, upg = ups
        cv = _causal_conv(upv, carry[:, vs], [conv_ref[k:k + 1, vs] for k in range(FFN_CONV)],
                          conv_ref[FFN_CONV:FFN_CONV + 1, vs])
        cg = _causal_conv(upg, carry[:, gs], [conv_ref[k:k + 1, gs] for k in range(FFN_CONV)],
                          conv_ref[FFN_CONV:FFN_CONV + 1, gs])
        carry[:, vs] = upv[tm - (FFN_CONV - 1) * BATCH:]
        carry[:, gs] = upg[tm - (FFN_CONV - 1) * BATCH:]
        act = (_gelu(cg) * cv).astype(_BF16)
        down = _dot(act, w_down_ref[vs, :])
        if j + 1 < n_chunks:
            acc[...] += down
        ups = nxt
    total = acc[...] + down
    if last:
        out = _rmsnorm(total, gains_ref[1:2, :])
        for s in range(D_MODEL // LANES):
            hslab[s] = out[:, s * LANES:(s + 1) * LANES]
        _from_time_major(hslab, o_ref, FFN_TILE_STEPS)
    else:
        o_ref[...] = total


def _ffn_call(h, ya, yr, consts, l, batch_major_in, last):
    tm = FFN_TILE_ROWS
    out_shape = (BATCH, SEQ, D_MODEL) if last else (ROWS, D_MODEL)
    return pl.pallas_call(
        functools.partial(_ffn_kernel, batch_major_in=batch_major_in, last=last),
        grid=(ROWS // tm,),
        in_specs=[_stream_spec(batch_major_in, FFN_TILE_STEPS, D_MODEL),
                  pl.BlockSpec((tm, MIX_A_WIDTH), lambda i: (i, 0)),
                  _stream_spec(True, FFN_TILE_STEPS, RET_WIDTH)]
                 + [_layer_spec(c, l) for c in consts],
        out_specs=_stream_spec(last, FFN_TILE_STEPS, D_MODEL),
        out_shape=jax.ShapeDtypeStruct(out_shape, _F32),
        scratch_shapes=[pltpu.VMEM((tm, D_MODEL), _F32),
                        pltpu.VMEM(((FFN_CONV - 1) * BATCH, 2 * D_FF), _F32),
                        pltpu.VMEM((RET_WIDTH // LANES, tm, LANES), _F32),
                        pltpu.VMEM((D_MODEL // LANES, tm, LANES), _F32)],
        compiler_params=pltpu.CompilerParams(dimension_semantics=("arbitrary",),
                                             vmem_limit_bytes=VMEM_LIMIT_BYTES),
        name="outproj_ffn",
    )(h, ya, yr, *consts)


def _pack_gate(w):
    blk = LRU_WIDTH // LRU_BLOCKS
    per = MXU_WIDTH // blk
    w5 = w.astype(_F32).reshape(DEPTH, LRU_BLOCKS // per, per, blk, blk)
    t = jnp.einsum('lphij,hg->lphigj', w5, jnp.eye(per, dtype=_F32))
    return t.reshape(DEPTH, LRU_BLOCKS // per, MXU_WIDTH, MXU_WIDTH).astype(_BF16)


def _s5_params(lam_re, lam_im, log_dt, b_re, b_im, c_re, c_im):
    T = S5_BLOCK
    dt = jnp.exp(log_dt.astype(_F32))[..., None]
    lr, li = lam_re.astype(_F32), lam_im.astype(_F32)
    mag = jnp.exp(lr * dt)
    abar_re, abar_im = mag * jnp.cos(li * dt), mag * jnp.sin(li * dt)
    den = lr * lr + li * li
    nr, ni = abar_re - 1.0, abar_im
    coef_re = (nr * lr + ni * li) / den
    coef_im = (ni * lr - nr * li) / den
    bt_re = jnp.swapaxes(b_re.astype(_F32), -1, -2)
    bt_im = jnp.swapaxes(b_im.astype(_F32), -1, -2)
    bbar_re = coef_re[:, :, None, :] * bt_re - coef_im[:, :, None, :] * bt_im
    bbar_im = coef_re[:, :, None, :] * bt_im + coef_im[:, :, None, :] * bt_re
    cr, ci = c_re.astype(_F32), c_im.astype(_F32)

    def apow(e):
        e = e.astype(_F32)
        m = jnp.exp(lr[:, :, None, :] * dt[:, :, None, :] * e)
        ang = li[:, :, None, :] * dt[:, :, None, :] * e
        return m * jnp.cos(ang), m * jnp.sin(ang)

    gidx = jnp.arange(S5_GROUPS)
    lag_of = (jnp.arange(T)[None, :] - (gidx % (LANES // S5_GROUP))[:, None]) % T
    slot = (jnp.arange(2 * LANES) // S5_STATE) % 2
    own = (slot[None, :] == (gidx % 2)[:, None]).astype(_F32)[None, :, None, None, :]
    wide = lambda re, im: jnp.concatenate([re, re, im, im], axis=-1)

    def pair_rows(x):
        return x.reshape(DEPTH, S5_GROUPS // 2, 2 * T * S5_GROUP, 2 * LANES).astype(_BF16)

    def cmul_tiles(p_re, p_im, q_re, q_im, conj_im=False):
        sgn = -1.0 if conj_im else 1.0
        a1, a2 = wide(p_re, p_re)[:, :, :, None, :], wide(p_im, p_im)[:, :, :, None, :]
        b1, b2 = wide(q_re, sgn * q_im)[:, :, None, :, :], wide(-q_im, sgn * q_re)[:, :, None, :, :]
        return (a1 * b1 + a2 * b2) * own

    w1 = pair_rows(cmul_tiles(*apow((T - 1 - lag_of)[None, :, :, None]), bbar_re, bbar_im))
    w2t = pair_rows(cmul_tiles(*apow((lag_of + 1)[None, :, :, None]), cr, ci, conj_im=True))
    p_re, p_im = apow(jnp.arange(T)[None, None, :, None])
    x_re = p_re[:, :, :, None, :] * bbar_re[:, :, None] - p_im[:, :, :, None, :] * bbar_im[:, :, None]
    x_im = p_re[:, :, :, None, :] * bbar_im[:, :, None] + p_im[:, :, :, None, :] * bbar_re[:, :, None]
    kt = jnp.einsum('lgtcp,lgop->lgtco', x_re, cr) - jnp.einsum('lgtcp,lgop->lgtco', x_im, ci)
    kexp = jnp.broadcast_to(kt[:, :, :, :, None, None, :], (DEPTH, S5_GROUPS, T, S5_GROUP, 2, T, S5_GROUP))
    kexp = kexp.reshape(DEPTH, S5_GROUPS, T, S5_GROUP, 2 * LANES)
    lag_col = jnp.repeat(jnp.concatenate([lag_of, lag_of], axis=1), S5_GROUP, axis=1)
    tau = lag_col[:, None, None, :] - lag_of[:, :, None, None]
    hit = (tau == jnp.arange(T)[None, None, :, None]).astype(_F32)
    own3 = (jnp.arange(2 * LANES) // LANES)[None, :] == (gidx % 2)[:, None]
    hit = hit * own3.astype(_F32)[:, None, None, :]
    w3 = pair_rows(jnp.sum(hit[None, :, :, :, None, :] * kexp[:, :, None, :, :, :], axis=3))

    a_re, a_im = apow(jnp.full((1, 1, 1, 1), T))
    a8 = jnp.concatenate([a_re.reshape(DEPTH, 1, S5_STATES), a_im.reshape(DEPTH, 1, S5_STATES)], axis=1)
    return a8, w1, w2t, w3


def kernel(x, positions, norm_mix, w_in, lru_conv_w, lru_conv_b, lru_wa, lru_ba, lru_wx, lru_bx, lru_lambda, lru_norm, s5_lambda_re, s5_lambda_im, s5_log_dt, s5_b_re, s5_b_im, s5_c_re, s5_c_im, s5_d, s5_w_glu, s5_b_glu, s5_norm, ret_norm, w_out, norm_ffn, w_up, ffn_conv_w, ffn_conv_b, w_down, norm_final):
    cos2, sin2 = _rotary_tables(positions)
    f32 = lambda v: v.astype(_F32)
    vec512 = jnp.concatenate(
        [f32(lru_conv_w),
         jnp.stack([f32(lru_conv_b), f32(lru_ba).reshape(DEPTH, LRU_WIDTH), f32(lru_bx).reshape(DEPTH, LRU_WIDTH),
                    jax.nn.softplus(-f32(lru_lambda)), f32(lru_norm), f32(s5_d), f32(s5_b_glu), f32(s5_norm)], axis=1),
         jnp.zeros((DEPTH, V_ROWS - V_S5_NORM - 1, LRU_WIDTH), _F32)], axis=1)
    a8, w1, w2, w3 = _s5_params(s5_lambda_re, s5_lambda_im, s5_log_dt, s5_b_re, s5_b_im, s5_c_re, s5_c_im)
    mix_consts = [f32(norm_mix).reshape(DEPTH, 1, D_MODEL), w_in.astype(_BF16), vec512,
                  _pack_gate(lru_wa), _pack_gate(lru_wx), a8, w1, w2, w3, s5_w_glu.astype(_BF16)]
    gains = jnp.stack([f32(norm_ffn), jnp.broadcast_to(f32(norm_final), (DEPTH, D_MODEL))], axis=1)
    ffn_conv = jnp.concatenate([f32(ffn_conv_w), f32(ffn_conv_b)[:, None, :]], axis=1)
    ffn_consts = [w_out.astype(_BF16), gains, w_up.astype(_BF16), ffn_conv, w_down.astype(_BF16)]
    gn_gain = f32(ret_norm).reshape(DEPTH, 1, RET_WIDTH)
    h = x
    for l in range(DEPTH):
        first, last = l == 0, l == DEPTH - 1
        ya, qkvg = _mix_call(h, mix_consts, l, batch_major_in=first)
        yr = _ret_call(qkvg, cos2, sin2, gn_gain, l)
        h = _ffn_call(h, ya, yr, ffn_consts, l, batch_major_in=first, last=last)
    return h
```

```python
import functools
import math

import jax
import jax.numpy as jnp
from jax import lax
from jax.experimental import pallas as pl
from jax.experimental.pallas import tpu as pltpu

D_MODEL = 1024
BATCH = 8
SEQ = 2048
DEPTH = 2
ROWS = BATCH * SEQ

LRU_WIDTH = 512
LRU_BLOCKS = 8
LRU_CONV = 4
LRU_C = 8.0
S5_WIDTH = 512
S5_GROUP = 16
S5_GROUPS = 32
S5_STATE = 64
S5_STATES = S5_GROUPS * S5_STATE
RET_HEADS = 4
RET_HEAD_DIM = 128
RET_WIDTH = 512
ROPE_BASE = 10000.0
MIX_A_WIDTH = LRU_WIDTH + S5_WIDTH
PROJ_A_WIDTH = 2 * LRU_WIDTH + S5_WIDTH
QKVG_WIDTH = 4 * RET_WIDTH
IN_WIDTH = PROJ_A_WIDTH + QKVG_WIDTH
D_FF = 3 * D_MODEL
FFN_CONV = 3
NORM_EPS = 1e-6

SUBLANES = 8
LANES = 128
MXU_WIDTH = 256
VMEM_LIMIT_BYTES = 56 * 1024 * 1024

MIX_TILE_STEPS = 64
MIX_TILE_ROWS = MIX_TILE_STEPS * BATCH
FFN_TILE_STEPS = 64
FFN_TILE_ROWS = FFN_TILE_STEPS * BATCH
FFN_CHUNK = 1024
RET_CHUNK = 256
S5_SCAN_LANES = 512
S5_BLOCK = 8

_F32 = jnp.float32
_BF16 = jnp.bfloat16

V_CONV_W, V_CONV_B, V_BA, V_BX, V_SOFTPLUS, V_LRU_NORM, V_S5_D, V_B_GLU, V_S5_NORM = 0, 4, 5, 6, 7, 8, 9, 10, 11
V_ROWS = 16


def _gelu(x):
    return 0.5 * x * (1.0 + jnp.tanh(0.7978845608028654 * (x + 0.044715 * (x * x * x))))


def _sigmoid(x):
    return 1.0 / (1.0 + jnp.exp(-x))


def _rmsnorm(x, gain):
    return x * lax.rsqrt(jnp.mean(x * x, axis=-1, keepdims=True) + NORM_EPS) * gain


def _dot(a, b):
    return jnp.dot(a, b, preferred_element_type=_F32)


def _to_time_major(x_ref, slab, steps):
    n = x_ref.shape[-1] // LANES
    for b in range(BATCH):
        for s in range(n):
            slab[s, pl.ds(b, steps, stride=BATCH), :] = x_ref[b, :, s * LANES:(s + 1) * LANES]
    return jnp.concatenate([slab[s] for s in range(n)], axis=-1)


def _from_time_major(slab, o_ref, steps):
    n = o_ref.shape[-1] // LANES
    for b in range(BATCH):
        for s in range(n):
            o_ref[b, :, s * LANES:(s + 1) * LANES] = (
                slab[s, pl.ds(b, steps, stride=BATCH), :].astype(o_ref.dtype))


def _causal_conv(x, hist, taps, bias):
    tm = x.shape[0]
    y = bias
    for k, w in enumerate(taps):
        back = (len(taps) - 1 - k) * BATCH
        xs = x if back == 0 else jnp.concatenate([hist[hist.shape[0] - back:], x[:tm - back]], axis=0)
        y = y + w * xs
    return y


def _rotary_kernel(pos_ref, inv_ref, sign_ref, cos_ref, sin_ref):
    half = RET_HEAD_DIM // 2
    pairs = SEQ // 2
    low = lax.broadcasted_iota(jnp.int32, (pairs, RET_HEAD_DIM), 1) < half
    pos = jnp.where(low, pos_ref[0, :, 0:1], pos_ref[0, :, 1:2]).astype(_F32)
    ang = pos * inv_ref[...]
    for table, out_ref, scale in ((jnp.cos(ang), cos_ref, None), (jnp.sin(ang), sin_ref, sign_ref[...])):
        swapped = pltpu.roll(table, half, 1)
        even = jnp.where(low, table, swapped)
        odd = jnp.where(low, swapped, table)
        if scale is not None:
            even, odd = even * scale, odd * scale
        out_ref[0, pl.ds(0, pairs, stride=2), :] = even
        out_ref[0, pl.ds(1, pairs, stride=2), :] = odd


def _rotary_tables(positions):
    half = RET_HEAD_DIM // 2
    inv = ROPE_BASE ** (-jnp.arange(half, dtype=_F32) * 2.0 / RET_HEAD_DIM)
    inv2 = jnp.concatenate([inv, inv])[None, :]
    sign = jnp.concatenate([-jnp.ones((half,), _F32), jnp.ones((half,), _F32)])[None, :]
    pos3 = positions.reshape(BATCH, SEQ // 2, 2)
    out = jax.ShapeDtypeStruct((BATCH, SEQ, RET_HEAD_DIM), _F32)
    return pl.pallas_call(
        _rotary_kernel,
        grid=(BATCH,),
        in_specs=[pl.BlockSpec((1, SEQ // 2, 2), lambda b: (b, 0, 0)),
                  pl.BlockSpec((1, RET_HEAD_DIM), lambda b: (0, 0)),
                  pl.BlockSpec((1, RET_HEAD_DIM), lambda b: (0, 0))],
        out_specs=[pl.BlockSpec((1, SEQ, RET_HEAD_DIM), lambda b: (b, 0, 0)),
                   pl.BlockSpec((1, SEQ, RET_HEAD_DIM), lambda b: (b, 0, 0))],
        out_shape=[out, out],
        name="rotary_tables",
    )(pos3, inv2, sign)


def _mix_kernel(h_ref, gain_ref, w_in_ref, vec_ref, wa_ref, wx_ref,
                a8_ref, w1_ref, w2_ref, w3_ref, wglu_ref,
                ya_ref, qkvg_ref,
                conv_hist, gate_s, u_s, la_s, lb_s, lru_state, sr, si, y_s, s5_state_r, s5_state_i, qslab, hslab,
                *, batch_major_in):
    tm = MIX_TILE_ROWS
    vec = lambda row: vec_ref[row:row + 1, :]

    @pl.when(pl.program_id(0) == 0)
    def _init():
        conv_hist[...] = jnp.zeros_like(conv_hist)
        lru_state[...] = jnp.zeros_like(lru_state)
        s5_state_r[...] = jnp.zeros_like(s5_state_r)
        s5_state_i[...] = jnp.zeros_like(s5_state_i)

    h = _to_time_major(h_ref, hslab, MIX_TILE_STEPS) if batch_major_in else h_ref[...]
    xn = _rmsnorm(h, gain_ref[...]).astype(_BF16)

    lru_x = _dot(xn, w_in_ref[:, 0:LRU_WIDTH])
    u_s[...] = _dot(xn, w_in_ref[:, 2 * LRU_WIDTH:PROJ_A_WIDTH])

    xc = _causal_conv(lru_x, conv_hist[...], [vec(V_CONV_W + k) for k in range(LRU_CONV)], vec(V_CONV_B))
    conv_hist[...] = lru_x[tm - (LRU_CONV - 1) * BATCH:]
    xcb = xc.astype(_BF16)
    halves = range(LRU_WIDTH // MXU_WIDTH)
    pre_r = jnp.concatenate([_dot(xcb[:, MXU_WIDTH * p:MXU_WIDTH * (p + 1)], wa_ref[p]) for p in halves], axis=-1)
    pre_i = jnp.concatenate([_dot(xcb[:, MXU_WIDTH * p:MXU_WIDTH * (p + 1)], wx_ref[p]) for p in halves], axis=-1)

    nb = MIX_TILE_STEPS // S5_BLOCK
    gps = LANES // S5_GROUP
    lane_blk = lax.broadcasted_iota(jnp.int32, (nb * BATCH, LANES), 1) // S5_GROUP
    u = u_s[...]
    lag_rows = [jnp.concatenate([u[(S5_BLOCK * k + i) * BATCH:(S5_BLOCK * k + i + 1) * BATCH, :]
                                 for k in range(nb)], axis=0) for i in range(S5_BLOCK)]
    u_grp = []
    for sb in range(S5_WIDTH // LANES):
        rot = [lag_rows[i][:, sb * LANES:(sb + 1) * LANES] for i in range(S5_BLOCK)]
        rot = [r if i == 0 else pltpu.roll(r, i * S5_GROUP, 1) for i, r in enumerate(rot)]
        rot = [r.astype(_BF16) for r in rot]
        for q in range(gps):
            o = rot[S5_BLOCK - 1]
            for i in range(S5_BLOCK - 2, -1, -1):
                o = jnp.where(lane_blk == (q + i) % gps, rot[i], o)
            u_grp.append(o)
    n_pair = S5_GROUPS // 2
    u_pair = [jnp.concatenate([u_grp[2 * m], u_grp[2 * m + 1]], axis=1) for m in range(n_pair)]

    for m in range(n_pair):
        v = _dot(u_pair[m], w1_ref[m])
        sr[:, LANES * m:LANES * (m + 1)] = v[:, 0:LANES]
        si[:, LANES * m:LANES * (m + 1)] = v[:, LANES:2 * LANES]

    r = _sigmoid(pre_r + vec(V_BA))
    i = _sigmoid(pre_i + vec(V_BX))
    log_a = (-LRU_C) * r * vec(V_SOFTPLUS)
    a = jnp.exp(log_a)
    la_s[...] = a
    z = -jnp.tanh(log_a) * (1.0 + a * a)
    lb_s[...] = jnp.where(z == 0.0, 0.0, z * lax.rsqrt(z)) * (i * xc)

    for sg in range(S5_STATES // S5_SCAN_LANES):
        sl = slice(sg * S5_SCAN_LANES, (sg + 1) * S5_SCAN_LANES)
        ar = jnp.broadcast_to(a8_ref[0:1, sl], (BATCH, S5_SCAN_LANES))
        ai = jnp.broadcast_to(a8_ref[1:2, sl], (BATCH, S5_SCAN_LANES))
        xr = s5_state_r[:, sl]
        xi = s5_state_i[:, sl]
        for k in range(nb):
            rows = slice(k * BATCH, (k + 1) * BATCH)
            vr, vi = sr[rows, sl], si[rows, sl]
            sr[rows, sl] = xr
            si[rows, sl] = xi
            xr, xi = ar * xr - ai * xi + vr, ar * xi + ai * xr + vi
        s5_state_r[:, sl] = xr
        s5_state_i[:, sl] = xi

    def project_qkvg(j):
        res = _dot(xn, w_in_ref[:, PROJ_A_WIDTH + j * 512:PROJ_A_WIDTH + (j + 1) * 512])
        for s in range(512 // LANES):
            qslab[j * (512 // LANES) + s] = res[:, s * LANES:(s + 1) * LANES]

    gate_s[...] = _dot(xn, w_in_ref[:, LRU_WIDTH:2 * LRU_WIDTH])
    project_qkvg(0)
    project_qkvg(1)

    hs = lru_state[...]
    for t in range(MIX_TILE_STEPS):
        rows = slice(t * BATCH, (t + 1) * BATCH)
        hs = la_s[rows, :] * hs + lb_s[rows, :]
        lb_s[rows, :] = hs
    lru_state[...] = hs

    y_grp = []
    for m in range(n_pair):
        s_cat = jnp.concatenate([sr[:, LANES * m:LANES * (m + 1)], si[:, LANES * m:LANES * (m + 1)]],
                                axis=1).astype(_BF16)
        ym = (lax.dot_general(s_cat, w2_ref[m], (((1,), (1,)), ((), ())), preferred_element_type=_F32)
              + _dot(u_pair[m], w3_ref[m]))
        y_grp += [ym[:, 0:LANES], ym[:, LANES:2 * LANES]]
    for sb in range(S5_WIDTH // LANES):
        for j in range(S5_BLOCK):
            t = y_grp[sb * gps + gps - 1]
            for q in range(gps - 2, -1, -1):
                t = jnp.where(lane_blk == (q + j) % gps, y_grp[sb * gps + q], t)
            if j:
                t = pltpu.roll(t, LANES - j * S5_GROUP, 1)
            for k in range(nb):
                row = (S5_BLOCK * k + j) * BATCH
                y_s[row:row + BATCH, sb * LANES:(sb + 1) * LANES] = t[k * BATCH:(k + 1) * BATCH, :]
    y = y_s[...] + vec(V_S5_D) * u_s[...]
    z = _gelu(y)
    glu = _dot(z.astype(_BF16), wglu_ref[...])

    y_lru = lb_s[...] * _gelu(gate_s[...])
    ya_ref[:, 0:LRU_WIDTH] = _rmsnorm(y_lru, vec(V_LRU_NORM)).astype(ya_ref.dtype)
    out = z * _sigmoid(glu + vec(V_B_GLU))
    ya_ref[:, LRU_WIDTH:MIX_A_WIDTH] = _rmsnorm(out, vec(V_S5_NORM)).astype(ya_ref.dtype)
    project_qkvg(2)
    project_qkvg(3)
    _from_time_major(qslab, qkvg_ref, MIX_TILE_STEPS)


def _layer_spec(arr, l):
    nd = arr.ndim
    return pl.BlockSpec((None,) + arr.shape[1:], lambda *_: (l,) + (0,) * (nd - 1),
                        pipeline_mode=pl.Buffered(1))


def _stream_spec(batch_major, steps, width):
    if batch_major:
        return pl.BlockSpec((BATCH, steps, width), lambda i: (0, i, 0))
    return pl.BlockSpec((steps * BATCH, width), lambda i: (i, 0))


def _mix_call(h, consts, l, batch_major_in):
    tm = MIX_TILE_ROWS
    return pl.pallas_call(
        functools.partial(_mix_kernel, batch_major_in=batch_major_in),
        grid=(ROWS // tm,),
        in_specs=[_stream_spec(batch_major_in, MIX_TILE_STEPS, D_MODEL)] + [_layer_spec(c, l) for c in consts],
        out_specs=[pl.BlockSpec((tm, MIX_A_WIDTH), lambda i: (i, 0)),
                   _stream_spec(True, MIX_TILE_STEPS, QKVG_WIDTH)],
        out_shape=[jax.ShapeDtypeStruct((ROWS, MIX_A_WIDTH), _BF16),
                   jax.ShapeDtypeStruct((BATCH, SEQ, QKVG_WIDTH), _BF16)],
        scratch_shapes=[pltpu.VMEM(((LRU_CONV - 1) * BATCH, LRU_WIDTH), _F32),
                        pltpu.VMEM((tm, LRU_WIDTH), _F32),
                        pltpu.VMEM((tm, S5_WIDTH), _F32),
                        pltpu.VMEM((tm, LRU_WIDTH), _F32),
                        pltpu.VMEM((tm, LRU_WIDTH), _F32),
                        pltpu.VMEM((BATCH, LRU_WIDTH), _F32),
                        pltpu.VMEM((tm // S5_BLOCK, S5_STATES), _F32),
                        pltpu.VMEM((tm // S5_BLOCK, S5_STATES), _F32),
                        pltpu.VMEM((tm, S5_WIDTH), _F32),
                        pltpu.VMEM((BATCH, S5_STATES), _F32),
                        pltpu.VMEM((BATCH, S5_STATES), _F32),
                        pltpu.VMEM((QKVG_WIDTH // LANES, tm, LANES), _F32),
                        pltpu.VMEM((D_MODEL // LANES, tm, LANES), _F32)],
        compiler_params=pltpu.CompilerParams(dimension_semantics=("arbitrary",),
                                             vmem_limit_bytes=VMEM_LIMIT_BYTES),
        name="mix_lru_s5",
    )(h, *consts)


def _ret_kernel(qkvg_ref, cos_ref, sin_ref, gn_ref, o_ref,
                state_ref, decay_ref, qdec_ref, kdec_ref):
    C = RET_CHUNK
    Dh = RET_HEAD_DIM
    log_gammas = [math.log1p(-(2.0 ** (-5.0 - h))) for h in range(RET_HEADS)]

    @pl.when((pl.program_id(0) == 0) & (pl.program_id(1) == 0))
    def _tables():
        row = lax.broadcasted_iota(jnp.int32, (C, C), 0)
        col = lax.broadcasted_iota(jnp.int32, (C, C), 1)
        rel = (row - col).astype(_F32)
        idx = lax.broadcasted_iota(jnp.int32, (C, Dh), 0).astype(_F32)
        for h in range(RET_HEADS):
            lg = log_gammas[h]
            decay_ref[h] = jnp.where(rel >= 0.0, jnp.exp(lg * jnp.maximum(rel, 0.0)), 0.0) * (Dh ** -0.5)
            qdec_ref[h] = jnp.exp(lg * (idx + 1.0))
            kdec_ref[h] = jnp.exp(lg * (C - 1.0 - idx)) * (Dh ** -0.5)

    @pl.when(pl.program_id(1) == 0)
    def _init():
        state_ref[...] = jnp.zeros_like(state_ref)

    cos2 = cos_ref[0]
    sin2 = sin_ref[0]

    def rot(t):
        return t * cos2 + pltpu.roll(t, Dh // 2, 1) * sin2

    for h in range(RET_HEADS):
        hs = slice(h * Dh, (h + 1) * Dh)
        qh = rot(qkvg_ref[:, h * Dh:(h + 1) * Dh].astype(_F32))
        kh = rot(qkvg_ref[:, RET_WIDTH + h * Dh:RET_WIDTH + (h + 1) * Dh].astype(_F32))
        vb = qkvg_ref[:, 2 * RET_WIDTH + h * Dh:2 * RET_WIDTH + (h + 1) * Dh]
        scores = lax.dot_general(qh.astype(_BF16), kh.astype(_BF16), (((1,), (1,)), ((), ())),
                                 preferred_element_type=_F32) * decay_ref[h]
        intra = _dot(scores.astype(_BF16), vb)
        state = state_ref[h]
        cross = _dot((qh * qdec_ref[h]).astype(_BF16), state.astype(_BF16))
        kv = lax.dot_general((kh * kdec_ref[h]).astype(_BF16), vb, (((0,), (0,)), ((), ())),
                             preferred_element_type=_F32)
        state_ref[h] = math.exp(log_gammas[h] * C) * state + kv
        o = intra + cross
        mu = jnp.mean(o, axis=-1, keepdims=True)
        oc = o - mu
        var = jnp.mean(oc * oc, axis=-1, keepdims=True)
        on = oc * lax.rsqrt(var + NORM_EPS) * gn_ref[:, hs]
        g = qkvg_ref[:, 3 * RET_WIDTH + h * Dh:3 * RET_WIDTH + (h + 1) * Dh].astype(_F32)
        o_ref[:, hs] = (on * (g * _sigmoid(g))).astype(o_ref.dtype)


def _ret_call(qkvg, cos2, sin2, gn_gain, l):
    C = RET_CHUNK
    return pl.pallas_call(
        _ret_kernel,
        grid=(BATCH, SEQ // C),
        in_specs=[pl.BlockSpec((None, C, QKVG_WIDTH), lambda b, n: (b, n, 0)),
                  pl.BlockSpec((1, C, RET_HEAD_DIM), lambda b, n: (b, n, 0)),
                  pl.BlockSpec((1, C, RET_HEAD_DIM), lambda b, n: (b, n, 0)),
                  pl.BlockSpec((None, 1, RET_WIDTH), lambda b, n: (l, 0, 0))],
        out_specs=pl.BlockSpec((None, C, RET_WIDTH), lambda b, n: (b, n, 0)),
        out_shape=jax.ShapeDtypeStruct((BATCH, SEQ, RET_WIDTH), _F32),
        scratch_shapes=[pltpu.VMEM((RET_HEADS, RET_HEAD_DIM, RET_HEAD_DIM), _F32),
                        pltpu.VMEM((RET_HEADS, C, C), _F32),
                        pltpu.VMEM((RET_HEADS, C, RET_HEAD_DIM), _F32),
                        pltpu.VMEM((RET_HEADS, C, RET_HEAD_DIM), _F32)],
        compiler_params=pltpu.CompilerParams(dimension_semantics=("arbitrary", "arbitrary"),
                                             vmem_limit_bytes=VMEM_LIMIT_BYTES),
        name="retention",
    )(qkvg, cos2, sin2, gn_gain)


def _ffn_kernel(h_ref, ya_ref, yr_ref, wo_ref, gains_ref, w_up_ref, conv_ref,
                w_down_ref, o_ref, acc, carry, yslab, hslab, *, batch_major_in, last):
    tm = FFN_TILE_ROWS
    fc = FFN_CHUNK

    @pl.when(pl.program_id(0) == 0)
    def _init():
        carry[...] = jnp.zeros_like(carry)

    h = _to_time_major(h_ref, hslab, FFN_TILE_STEPS) if batch_major_in else h_ref[...]
    yr = _to_time_major(yr_ref, yslab, FFN_TILE_STEPS).astype(_BF16)
    h1 = _dot(ya_ref[...], wo_ref[0:MIX_A_WIDTH, :]) + _dot(yr, wo_ref[MIX_A_WIDTH:, :]) + h
    acc[...] = h1
    xn = _rmsnorm(h1, gains_ref[0:1, :]).astype(_BF16)

    def up_proj(j):
        return (_dot(xn, w_up_ref[:, j * fc:(j + 1) * fc]),
                _dot(xn, w_up_ref[:, D_FF + j * fc:D_FF + (j + 1) * fc]))

    n_chunks = D_FF // fc
    ups = up_proj(0)
    for j in range(n_chunks):
        nxt = up_proj(j + 1) if j + 1 < n_chunks else None
        vs = slice(j * fc, (j + 1) * fc)
        gs = slice(D_FF + j * fc, D_FF + (j + 1) * fc)
        upv, upg = ups
        cv = _causal_conv(upv, carry[:, vs], [conv_ref[k:k + 1, vs] for k in range(FFN_CONV)],
                          conv_ref[FFN_CONV:FFN_CONV + 1, vs])
        cg = _causal_conv(upg, carry[:, gs], [conv_ref[k:k + 1, gs] for k in range(FFN_CONV)],
                          conv_ref[FFN_CONV:FFN_CONV + 1, gs])
        carry[:, vs] = upv[tm - (FFN_CONV - 1) * BATCH:]
        carry[:, gs] = upg[tm - (FFN_CONV - 1) * BATCH:]
        act = (_gelu(cg) * cv).astype(_BF16)
        down = _dot(act, w_down_ref[vs, :])
        if j + 1 < n_chunks:
            acc[...] += down
        ups = nxt
    total = acc[...] + down
    if last:
        out = _rmsnorm(total, gains_ref[1:2, :])
        for s in range(D_MODEL // LANES):
            hslab[s] = out[:, s * LANES:(s + 1) * LANES]
        _from_time_major(hslab, o_ref, FFN_TILE_STEPS)
    else:
        o_ref[...] = total


def _ffn_call(h, ya, yr, consts, l, batch_major_in, last):
    tm = FFN_TILE_ROWS
    out_shape = (BATCH, SEQ, D_MODEL) if last else (ROWS, D_MODEL)
    return pl.pallas_call(
        functools.partial(_ffn_kernel, batch_major_in=batch_major_in, last=last),
        grid=(ROWS // tm,),
        in_specs=[_stream_spec(batch_major_in, FFN_TILE_STEPS, D_MODEL),
                  pl.BlockSpec((tm, MIX_A_WIDTH), lambda i: (i, 0)),
                  _stream_spec(True, FFN_TILE_STEPS, RET_WIDTH)]
                 + [_layer_spec(c, l) for c in consts],
        out_specs=_stream_spec(last, FFN_TILE_STEPS, D_MODEL),
        out_shape=jax.ShapeDtypeStruct(out_shape, _F32),
        scratch_shapes=[pltpu.VMEM((tm, D_MODEL), _F32),
                        pltpu.VMEM(((FFN_CONV - 1) * BATCH, 2 * D_FF), _F32),
                        pltpu.VMEM((RET_WIDTH // LANES, tm, LANES), _F32),
                        pltpu.VMEM((D_MODEL // LANES, tm, LANES), _F32)],
        compiler_params=pltpu.CompilerParams(dimension_semantics=("arbitrary",),
                                             vmem_limit_bytes=VMEM_LIMIT_BYTES),
        name="outproj_ffn",
    )(h, ya, yr, *consts)


def _pack_gate(w):
    blk = LRU_WIDTH // LRU_BLOCKS
    per = MXU_WIDTH // blk
    w5 = w.astype(_F32).reshape(DEPTH, LRU_BLOCKS // per, per, blk, blk)
    t = jnp.einsum('lphij,hg->lphigj', w5, jnp.eye(per, dtype=_F32))
    return t.reshape(DEPTH, LRU_BLOCKS // per, MXU_WIDTH, MXU_WIDTH).astype(_BF16)


def _s5_params(lam_re, lam_im, log_dt, b_re, b_im, c_re, c_im):
    T = S5_BLOCK
    dt = jnp.exp(log_dt.astype(_F32))[..., None]
    lr, li = lam_re.astype(_F32), lam_im.astype(_F32)
    mag = jnp.exp(lr * dt)
    abar_re, abar_im = mag * jnp.cos(li * dt), mag * jnp.sin(li * dt)
    den = lr * lr + li * li
    nr, ni = abar_re - 1.0, abar_im
    coef_re = (nr * lr + ni * li) / den
    coef_im = (ni * lr - nr * li) / den
    bt_re = jnp.swapaxes(b_re.astype(_F32), -1, -2)
    bt_im = jnp.swapaxes(b_im.astype(_F32), -1, -2)
    bbar_re = coef_re[:, :, None, :] * bt_re - coef_im[:, :, None, :] * bt_im
    bbar_im = coef_re[:, :, None, :] * bt_im + coef_im[:, :, None, :] * bt_re
    cr, ci = c_re.astype(_F32), c_im.astype(_F32)

    def apow(e):
        e = e.astype(_F32)
        m = jnp.exp(lr[:, :, None, :] * dt[:, :, None, :] * e)
        ang = li[:, :, None, :] * dt[:, :, None, :] * e
        return m * jnp.cos(ang), m * jnp.sin(ang)

    gidx = jnp.arange(S5_GROUPS)
    lag_of = (jnp.arange(T)[None, :] - (gidx % (LANES // S5_GROUP))[:, None]) % T
    slot = (jnp.arange(2 * LANES) // S5_STATE) % 2
    own = (slot[None, :] == (gidx % 2)[:, None]).astype(_F32)[None, :, None, None, :]
    wide = lambda re, im: jnp.concatenate([re, re, im, im], axis=-1)

    def pair_rows(x):
        return x.reshape(DEPTH, S5_GROUPS // 2, 2 * T * S5_GROUP, 2 * LANES).astype(_BF16)

    def cmul_tiles(p_re, p_im, q_re, q_im, conj_im=False):
        sgn = -1.0 if conj_im else 1.0
        a1, a2 = wide(p_re, p_re)[:, :, :, None, :], wide(p_im, p_im)[:, :, :, None, :]
        b1, b2 = wide(q_re, sgn * q_im)[:, :, None, :, :], wide(-q_im, sgn * q_re)[:, :, None, :, :]
        return (a1 * b1 + a2 * b2) * own

    w1 = pair_rows(cmul_tiles(*apow((T - 1 - lag_of)[None, :, :, None]), bbar_re, bbar_im))
    w2t = pair_rows(cmul_tiles(*apow((lag_of + 1)[None, :, :, None]), cr, ci, conj_im=True))
    p_re, p_im = apow(jnp.arange(T)[None, None, :, None])
    x_re = p_re[:, :, :, None, :] * bbar_re[:, :, None] - p_im[:, :, :, None, :] * bbar_im[:, :, None]
    x_im = p_re[:, :, :, None, :] * bbar_im[:, :, None] + p_im[:, :, :, None, :] * bbar_re[:, :, None]
    kt = jnp.einsum('lgtcp,lgop->lgtco', x_re, cr) - jnp.einsum('lgtcp,lgop->lgtco', x_im, ci)
    expand = (jnp.arange(2 * LANES)[None, :] % S5_GROUP == jnp.arange(S5_GROUP)[:, None]).astype(_F32)
    kexp = jnp.einsum('lgtco,on->lgtcn', kt, expand)
    lag_col = jnp.repeat(jnp.concatenate([lag_of, lag_of], axis=1), S5_GROUP, axis=1)
    tau = lag_col[:, None, None, :] - lag_of[:, :, None, None]
    hit = (tau == jnp.arange(T)[None, None, :, None]).astype(_F32)
    own3 = (jnp.arange(2 * LANES) // LANES)[None, :] == (gidx % 2)[:, None]
    hit = hit * own3.astype(_F32)[:, None, None, :]
    w3 = pair_rows(jnp.sum(hit[None, :, :, :, None, :] * kexp[:, :, None, :, :, :], axis=3))

    a_re, a_im = apow(jnp.full((1, 1, 1, 1), T))
    a8 = jnp.concatenate([a_re.reshape(DEPTH, 1, S5_STATES), a_im.reshape(DEPTH, 1, S5_STATES)], axis=1)
    return a8, w1, w2t, w3


def kernel(x, positions, norm_mix, w_in, lru_conv_w, lru_conv_b, lru_wa, lru_ba, lru_wx, lru_bx, lru_lambda, lru_norm, s5_lambda_re, s5_lambda_im, s5_log_dt, s5_b_re, s5_b_im, s5_c_re, s5_c_im, s5_d, s5_w_glu, s5_b_glu, s5_norm, ret_norm, w_out, norm_ffn, w_up, ffn_conv_w, ffn_conv_b, w_down, norm_final):
    cos2, sin2 = _rotary_tables(positions)
    f32 = lambda v: v.astype(_F32)
    vec512 = jnp.concatenate(
        [f32(lru_conv_w),
         jnp.stack([f32(lru_conv_b), f32(lru_ba).reshape(DEPTH, LRU_WIDTH), f32(lru_bx).reshape(DEPTH, LRU_WIDTH),
                    jax.nn.softplus(-f32(lru_lambda)), f32(lru_norm), f32(s5_d), f32(s5_b_glu), f32(s5_norm)], axis=1),
         jnp.zeros((DEPTH, V_ROWS - V_S5_NORM - 1, LRU_WIDTH), _F32)], axis=1)
    a8, w1, w2, w3 = _s5_params(s5_lambda_re, s5_lambda_im, s5_log_dt, s5_b_re, s5_b_im, s5_c_re, s5_c_im)
    mix_consts = [f32(norm_mix).reshape(DEPTH, 1, D_MODEL), w_in.astype(_BF16), vec512,
                  _pack_gate(lru_wa), _pack_gate(lru_wx), a8, w1, w2, w3, s5_w_glu.astype(_BF16)]
    gains = jnp.stack([f32(norm_ffn), jnp.broadcast_to(f32(norm_final), (DEPTH, D_MODEL))], axis=1)
    ffn_conv = jnp.concatenate([f32(ffn_conv_w), f32(ffn_conv_b)[:, None, :]], axis=1)
    ffn_consts = [w_out.astype(_BF16), gains, w_up.astype(_BF16), ffn_conv, w_down.astype(_BF16)]
    gn_gain = f32(ret_norm).reshape(DEPTH, 1, RET_WIDTH)
    h = x
    for l in range(DEPTH):
        first, last = l == 0, l == DEPTH - 1
        ya, qkvg = _mix_call(h, mix_consts, l, batch_major_in=first)
        yr = _ret_call(qkvg, cos2, sin2, gn_gain, l)
        h = _ffn_call(h, ya, yr, ffn_consts, l, batch_major_in=first, last=last)
    return h
```

```python
import functools
import math

import jax
import jax.numpy as jnp
from jax import lax
from jax.experimental import pallas as pl
from jax.experimental.pallas import tpu as pltpu

D_MODEL = 1024
BATCH = 8
SEQ = 2048
DEPTH = 2
ROWS = BATCH * SEQ

LRU_WIDTH = 512
LRU_BLOCKS = 8
LRU_CONV = 4
LRU_C = 8.0
S5_WIDTH = 512
S5_GROUP = 16
S5_GROUPS = 32
S5_STATE = 64
S5_STATES = S5_GROUPS * S5_STATE
RET_HEADS = 4
RET_HEAD_DIM = 128
RET_WIDTH = 512
ROPE_BASE = 10000.0
MIX_A_WIDTH = LRU_WIDTH + S5_WIDTH
PROJ_A_WIDTH = 2 * LRU_WIDTH + S5_WIDTH
QKVG_WIDTH = 4 * RET_WIDTH
IN_WIDTH = PROJ_A_WIDTH + QKVG_WIDTH
D_FF = 3 * D_MODEL
FFN_CONV = 3
NORM_EPS = 1e-6

SUBLANES = 8
LANES = 128
MXU_WIDTH = 256
VMEM_LIMIT_BYTES = 56 * 1024 * 1024

MIX_TILE_STEPS = 64
MIX_TILE_ROWS = MIX_TILE_STEPS * BATCH
FFN_TILE_STEPS = 64
FFN_TILE_ROWS = FFN_TILE_STEPS * BATCH
FFN_CHUNK = 1024
RET_CHUNK = 256
S5_SCAN_LANES = 512
S5_BLOCK = 8

_F32 = jnp.float32
_BF16 = jnp.bfloat16

V_CONV_W, V_CONV_B, V_BA, V_BX, V_SOFTPLUS, V_LRU_NORM, V_S5_D, V_B_GLU, V_S5_NORM = 0, 4, 5, 6, 7, 8, 9, 10, 11
V_ROWS = 16


def _gelu(x):
    return 0.5 * x * (1.0 + jnp.tanh(0.7978845608028654 * (x + 0.044715 * (x * x * x))))


def _sigmoid(x):
    return 1.0 / (1.0 + jnp.exp(-x))


def _rmsnorm(x, gain):
    return x * lax.rsqrt(jnp.mean(x * x, axis=-1, keepdims=True) + NORM_EPS) * gain


def _dot(a, b):
    return jnp.dot(a, b, preferred_element_type=_F32)


def _to_time_major(x_ref, slab, steps):
    n = x_ref.shape[-1] // LANES
    for b in range(BATCH):
        for s in range(n):
            slab[s, pl.ds(b, steps, stride=BATCH), :] = x_ref[b, :, s * LANES:(s + 1) * LANES]
    return jnp.concatenate([slab[s] for s in range(n)], axis=-1)


def _from_time_major(slab, o_ref, steps):
    n = o_ref.shape[-1] // LANES
    for b in range(BATCH):
        for s in range(n):
            o_ref[b, :, s * LANES:(s + 1) * LANES] = (
                slab[s, pl.ds(b, steps, stride=BATCH), :].astype(o_ref.dtype))


def _causal_conv(x, hist, taps, bias):
    tm = x.shape[0]
    y = bias
    for k, w in enumerate(taps):
        back = (len(taps) - 1 - k) * BATCH
        xs = x if back == 0 else jnp.concatenate([hist[hist.shape[0] - back:], x[:tm - back]], axis=0)
        y = y + w * xs
    return y


def _rotary_kernel(pos_ref, inv_ref, sign_ref, cos_ref, sin_ref):
    half = RET_HEAD_DIM // 2
    pairs = SEQ // 2
    low = lax.broadcasted_iota(jnp.int32, (pairs, RET_HEAD_DIM), 1) < half
    pos = jnp.where(low, pos_ref[0, :, 0:1], pos_ref[0, :, 1:2]).astype(_F32)
    ang = pos * inv_ref[...]
    for table, out_ref, scale in ((jnp.cos(ang), cos_ref, None), (jnp.sin(ang), sin_ref, sign_ref[...])):
        swapped = pltpu.roll(table, half, 1)
        even = jnp.where(low, table, swapped)
        odd = jnp.where(low, swapped, table)
        if scale is not None:
            even, odd = even * scale, odd * scale
        out_ref[0, pl.ds(0, pairs, stride=2), :] = even
        out_ref[0, pl.ds(1, pairs, stride=2), :] = odd


def _rotary_tables(positions):
    half = RET_HEAD_DIM // 2
    inv = ROPE_BASE ** (-jnp.arange(half, dtype=_F32) * 2.0 / RET_HEAD_DIM)
    inv2 = jnp.concatenate([inv, inv])[None, :]
    sign = jnp.concatenate([-jnp.ones((half,), _F32), jnp.ones((half,), _F32)])[None, :]
    pos3 = positions.reshape(BATCH, SEQ // 2, 2)
    out = jax.ShapeDtypeStruct((BATCH, SEQ, RET_HEAD_DIM), _F32)
    return pl.pallas_call(
        _rotary_kernel,
        grid=(BATCH,),
        in_specs=[pl.BlockSpec((1, SEQ // 2, 2), lambda b: (b, 0, 0)),
                  pl.BlockSpec((1, RET_HEAD_DIM), lambda b: (0, 0)),
                  pl.BlockSpec((1, RET_HEAD_DIM), lambda b: (0, 0))],
        out_specs=[pl.BlockSpec((1, SEQ, RET_HEAD_DIM), lambda b: (b, 0, 0)),
                   pl.BlockSpec((1, SEQ, RET_HEAD_DIM), lambda b: (b, 0, 0))],
        out_shape=[out, out],
        name="rotary_tables",
    )(pos3, inv2, sign)


def _mix_kernel(h_ref, gain_ref, w_in_ref, vec_ref, wa_ref, wx_ref,
                a8_ref, w1_ref, w2_ref, w3_ref, wglu_ref,
                ya_ref, qkvg_ref,
                conv_hist, gate_s, u_s, la_s, lb_s, lru_state, sr, si, y_s, s5_state_r, s5_state_i, qslab, hslab,
                *, batch_major_in):
    tm = MIX_TILE_ROWS
    vec = lambda row: vec_ref[row:row + 1, :]

    @pl.when(pl.program_id(0) == 0)
    def _init():
        conv_hist[...] = jnp.zeros_like(conv_hist)
        lru_state[...] = jnp.zeros_like(lru_state)
        s5_state_r[...] = jnp.zeros_like(s5_state_r)
        s5_state_i[...] = jnp.zeros_like(s5_state_i)

    h = _to_time_major(h_ref, hslab, MIX_TILE_STEPS) if batch_major_in else h_ref[...]
    xn = _rmsnorm(h, gain_ref[...]).astype(_BF16)

    lru_x = _dot(xn, w_in_ref[:, 0:LRU_WIDTH])
    u_s[...] = _dot(xn, w_in_ref[:, 2 * LRU_WIDTH:PROJ_A_WIDTH])

    xc = _causal_conv(lru_x, conv_hist[...], [vec(V_CONV_W + k) for k in range(LRU_CONV)], vec(V_CONV_B))
    conv_hist[...] = lru_x[tm - (LRU_CONV - 1) * BATCH:]
    xcb = xc.astype(_BF16)
    halves = range(LRU_WIDTH // MXU_WIDTH)
    pre_r = jnp.concatenate([_dot(xcb[:, MXU_WIDTH * p:MXU_WIDTH * (p + 1)], wa_ref[p]) for p in halves], axis=-1)
    pre_i = jnp.concatenate([_dot(xcb[:, MXU_WIDTH * p:MXU_WIDTH * (p + 1)], wx_ref[p]) for p in halves], axis=-1)

    nb = MIX_TILE_STEPS // S5_BLOCK
    gps = LANES // S5_GROUP
    lane_blk = lax.broadcasted_iota(jnp.int32, (nb * BATCH, LANES), 1) // S5_GROUP
    u = u_s[...]
    lag_rows = [jnp.concatenate([u[(S5_BLOCK * k + i) * BATCH:(S5_BLOCK * k + i + 1) * BATCH, :]
                                 for k in range(nb)], axis=0) for i in range(S5_BLOCK)]
    u_grp = []
    for sb in range(S5_WIDTH // LANES):
        rot = [lag_rows[i][:, sb * LANES:(sb + 1) * LANES] for i in range(S5_BLOCK)]
        rot = [r if i == 0 else pltpu.roll(r, i * S5_GROUP, 1) for i, r in enumerate(rot)]
        rot = [r.astype(_BF16) for r in rot]
        for q in range(gps):
            o = rot[S5_BLOCK - 1]
            for i in range(S5_BLOCK - 2, -1, -1):
                o = jnp.where(lane_blk == (q + i) % gps, rot[i], o)
            u_grp.append(o)
    n_pair = S5_GROUPS // 2
    u_pair = [jnp.concatenate([u_grp[2 * m], u_grp[2 * m + 1]], axis=1) for m in range(n_pair)]

    for m in range(n_pair):
        v = _dot(u_pair[m], w1_ref[m])
        sr[:, LANES * m:LANES * (m + 1)] = v[:, 0:LANES]
        si[:, LANES * m:LANES * (m + 1)] = v[:, LANES:2 * LANES]

    r = _sigmoid(pre_r + vec(V_BA))
    i = _sigmoid(pre_i + vec(V_BX))
    log_a = (-LRU_C) * r * vec(V_SOFTPLUS)
    a = jnp.exp(log_a)
    la_s[...] = a
    z = -jnp.tanh(log_a) * (1.0 + a * a)
    lb_s[...] = jnp.where(z == 0.0, 0.0, z * lax.rsqrt(z)) * (i * xc)

    for sg in range(S5_STATES // S5_SCAN_LANES):
        sl = slice(sg * S5_SCAN_LANES, (sg + 1) * S5_SCAN_LANES)
        ar = jnp.broadcast_to(a8_ref[0:1, sl], (BATCH, S5_SCAN_LANES))
        ai = jnp.broadcast_to(a8_ref[1:2, sl], (BATCH, S5_SCAN_LANES))
        xr = s5_state_r[:, sl]
        xi = s5_state_i[:, sl]
        for k in range(nb):
            rows = slice(k * BATCH, (k + 1) * BATCH)
            vr, vi = sr[rows, sl], si[rows, sl]
            sr[rows, sl] = xr
            si[rows, sl] = xi
            xr, xi = ar * xr - ai * xi + vr, ar * xi + ai * xr + vi
        s5_state_r[:, sl] = xr
        s5_state_i[:, sl] = xi

    def project_qkvg(j):
        res = _dot(xn, w_in_ref[:, PROJ_A_WIDTH + j * 512:PROJ_A_WIDTH + (j + 1) * 512])
        for s in range(512 // LANES):
            qslab[j * (512 // LANES) + s] = res[:, s * LANES:(s + 1) * LANES]

    gate_s[...] = _dot(xn, w_in_ref[:, LRU_WIDTH:2 * LRU_WIDTH])
    project_qkvg(0)
    project_qkvg(1)

    hs = lru_state[...]
    for t in range(MIX_TILE_STEPS):
        rows = slice(t * BATCH, (t + 1) * BATCH)
        hs = la_s[rows, :] * hs + lb_s[rows, :]
        lb_s[rows, :] = hs
    lru_state[...] = hs

    y_grp = []
    for m in range(n_pair):
        s_cat = jnp.concatenate([sr[:, LANES * m:LANES * (m + 1)], si[:, LANES * m:LANES * (m + 1)]],
                                axis=1).astype(_BF16)
        ym = (lax.dot_general(s_cat, w2_ref[m], (((1,), (1,)), ((), ())), preferred_element_type=_F32)
              + _dot(u_pair[m], w3_ref[m]))
        y_grp += [ym[:, 0:LANES], ym[:, LANES:2 * LANES]]
    for sb in range(S5_WIDTH // LANES):
        for j in range(S5_BLOCK):
            t = y_grp[sb * gps + gps - 1]
            for q in range(gps - 2, -1, -1):
                t = jnp.where(lane_blk == (q + j) % gps, y_grp[sb * gps + q], t)
            if j:
                t = pltpu.roll(t, LANES - j * S5_GROUP, 1)
            for k in range(nb):
                row = (S5_BLOCK * k + j) * BATCH
                y_s[row:row + BATCH, sb * LANES:(sb + 1) * LANES] = t[k * BATCH:(k + 1) * BATCH, :]
    y = y_s[...] + vec(V_S5_D) * u_s[...]
    z = _gelu(y)
    glu = _dot(z.astype(_BF16), wglu_ref[...])

    y_lru = lb_s[...] * _gelu(gate_s[...])
    ya_ref[:, 0:LRU_WIDTH] = _rmsnorm(y_lru, vec(V_LRU_NORM)).astype(ya_ref.dtype)
    out = z * _sigmoid(glu + vec(V_B_GLU))
    ya_ref[:, LRU_WIDTH:MIX_A_WIDTH] = _rmsnorm(out, vec(V_S5_NORM)).astype(ya_ref.dtype)
    project_qkvg(2)
    project_qkvg(3)
    _from_time_major(qslab, qkvg_ref, MIX_TILE_STEPS)


def _layer_spec(arr, l):
    nd = arr.ndim
    return pl.BlockSpec((None,) + arr.shape[1:], lambda *_: (l,) + (0,) * (nd - 1),
                        pipeline_mode=pl.Buffered(1))


def _stream_spec(batch_major, steps, width):
    if batch_major:
        return pl.BlockSpec((BATCH, steps, width), lambda i: (0, i, 0))
    return pl.BlockSpec((steps * BATCH, width), lambda i: (i, 0))


def _mix_call(h, consts, l, batch_major_in):
    tm = MIX_TILE_ROWS
    return pl.pallas_call(
        functools.partial(_mix_kernel, batch_major_in=batch_major_in),
        grid=(ROWS // tm,),
        in_specs=[_stream_spec(batch_major_in, MIX_TILE_STEPS, D_MODEL)] + [_layer_spec(c, l) for c in consts],
        out_specs=[pl.BlockSpec((tm, MIX_A_WIDTH), lambda i: (i, 0)),
                   _stream_spec(True, MIX_TILE_STEPS, QKVG_WIDTH)],
        out_shape=[jax.ShapeDtypeStruct((ROWS, MIX_A_WIDTH), _BF16),
                   jax.ShapeDtypeStruct((BATCH, SEQ, QKVG_WIDTH), _BF16)],
        scratch_shapes=[pltpu.VMEM(((LRU_CONV - 1) * BATCH, LRU_WIDTH), _F32),
                        pltpu.VMEM((tm, LRU_WIDTH), _F32),
                        pltpu.VMEM((tm, S5_WIDTH), _F32),
                        pltpu.VMEM((tm, LRU_WIDTH), _F32),
                        pltpu.VMEM((tm, LRU_WIDTH), _F32),
                        pltpu.VMEM((BATCH, LRU_WIDTH), _F32),
                        pltpu.VMEM((tm // S5_BLOCK, S5_STATES), _F32),
                        pltpu.VMEM((tm // S5_BLOCK, S5_STATES), _F32),
                        pltpu.VMEM((tm, S5_WIDTH), _F32),
                        pltpu.VMEM((BATCH, S5_STATES), _F32),
                        pltpu.VMEM((BATCH, S5_STATES), _F32),
                        pltpu.VMEM((QKVG_WIDTH // LANES, tm, LANES), _F32),
                        pltpu.VMEM((D_MODEL // LANES, tm, LANES), _F32)],
        compiler_params=pltpu.CompilerParams(dimension_semantics=("arbitrary",),
                                             vmem_limit_bytes=VMEM_LIMIT_BYTES),
        name="mix_lru_s5",
    )(h, *consts)


def _ret_kernel(qkvg_ref, cos_ref, sin_ref, gn_ref, o_ref,
                state_ref, decay_ref, qdec_ref, kdec_ref):
    C = RET_CHUNK
    Dh = RET_HEAD_DIM
    log_gammas = [math.log1p(-(2.0 ** (-5.0 - h))) for h in range(RET_HEADS)]

    @pl.when((pl.program_id(0) == 0) & (pl.program_id(1) == 0))
    def _tables():
        row = lax.broadcasted_iota(jnp.int32, (C, C), 0)
        col = lax.broadcasted_iota(jnp.int32, (C, C), 1)
        rel = (row - col).astype(_F32)
        idx = lax.broadcasted_iota(jnp.int32, (C, Dh), 0).astype(_F32)
        for h in range(RET_HEADS):
            lg = log_gammas[h]
            decay_ref[h] = jnp.where(rel >= 0.0, jnp.exp(lg * jnp.maximum(rel, 0.0)), 0.0) * (Dh ** -0.5)
            qdec_ref[h] = jnp.exp(lg * (idx + 1.0))
            kdec_ref[h] = jnp.exp(lg * (C - 1.0 - idx)) * (Dh ** -0.5)

    @pl.when(pl.program_id(1) == 0)
    def _init():
        state_ref[...] = jnp.zeros_like(state_ref)

    cos2 = cos_ref[0]
    sin2 = sin_ref[0]

    def rot(t):
        return t * cos2 + pltpu.roll(t, Dh // 2, 1) * sin2

    for h in range(RET_HEADS):
        hs = slice(h * Dh, (h + 1) * Dh)
        qh = rot(qkvg_ref[:, h * Dh:(h + 1) * Dh].astype(_F32))
        kh = rot(qkvg_ref[:, RET_WIDTH + h * Dh:RET_WIDTH + (h + 1) * Dh].astype(_F32))
        vb = qkvg_ref[:, 2 * RET_WIDTH + h * Dh:2 * RET_WIDTH + (h + 1) * Dh]
        scores = lax.dot_general(qh.astype(_BF16), kh.astype(_BF16), (((1,), (1,)), ((), ())),
                                 preferred_element_type=_F32) * decay_ref[h]
        intra = _dot(scores.astype(_BF16), vb)
        state = state_ref[h]
        cross = _dot((qh * qdec_ref[h]).astype(_BF16), state.astype(_BF16))
        kv = lax.dot_general((kh * kdec_ref[h]).astype(_BF16), vb, (((0,), (0,)), ((), ())),
                             preferred_element_type=_F32)
        state_ref[h] = math.exp(log_gammas[h] * C) * state + kv
        o = intra + cross
        mu = jnp.mean(o, axis=-1, keepdims=True)
        oc = o - mu
        var = jnp.mean(oc * oc, axis=-1, keepdims=True)
        on = oc * lax.rsqrt(var + NORM_EPS) * gn_ref[:, hs]
        g = qkvg_ref[:, 3 * RET_WIDTH + h * Dh:3 * RET_WIDTH + (h + 1) * Dh].astype(_F32)
        o_ref[:, hs] = (on * (g * _sigmoid(g))).astype(o_ref.dtype)


def _ret_call(qkvg, cos2, sin2, gn_gain, l):
    C = RET_CHUNK
    return pl.pallas_call(
        _ret_kernel,
        grid=(BATCH, SEQ // C),
        in_specs=[pl.BlockSpec((None, C, QKVG_WIDTH), lambda b, n: (b, n, 0)),
                  pl.BlockSpec((1, C, RET_HEAD_DIM), lambda b, n: (b, n, 0)),
                  pl.BlockSpec((1, C, RET_HEAD_DIM), lambda b, n: (b, n, 0)),
                  pl.BlockSpec((None, 1, RET_WIDTH), lambda b, n: (l, 0, 0))],
        out_specs=pl.BlockSpec((None, C, RET_WIDTH), lambda b, n: (b, n, 0)),
        out_shape=jax.ShapeDtypeStruct((BATCH, SEQ, RET_WIDTH), _F32),
        scratch_shapes=[pltpu.VMEM((RET_HEADS, RET_HEAD_DIM, RET_HEAD_DIM), _F32),
                        pltpu.VMEM((RET_HEADS, C, C), _F32),
                        pltpu.VMEM((RET_HEADS, C, RET_HEAD_DIM), _F32),
                        pltpu.VMEM((RET_HEADS, C, RET_HEAD_DIM), _F32)],
        compiler_params=pltpu.CompilerParams(dimension_semantics=("arbitrary", "arbitrary"),
                                             vmem_limit_bytes=VMEM_LIMIT_BYTES),
        name="retention",
    )(qkvg, cos2, sin2, gn_gain)


def _ffn_kernel(h_ref, ya_ref, yr_ref, wo_ref, gains_ref, w_up_ref, conv_ref,
                w_down_ref, o_ref, acc, carry, yslab, hslab, *, batch_major_in, last):
    tm = FFN_TILE_ROWS
    fc = FFN_CHUNK

    @pl.when(pl.program_id(0) == 0)
    def _init():
        carry[...] = jnp.zeros_like(carry)

    h = _to_time_major(h_ref, hslab, FFN_TILE_STEPS) if batch_major_in else h_ref[...]
    yr = _to_time_major(yr_ref, yslab, FFN_TILE_STEPS).astype(_BF16)
    h1 = _dot(ya_ref[...], wo_ref[0:MIX_A_WIDTH, :]) + _dot(yr, wo_ref[MIX_A_WIDTH:, :]) + h
    acc[...] = h1
    xn = _rmsnorm(h1, gains_ref[0:1, :]).astype(_BF16)

    def up_proj(j):
        return (_dot(xn, w_up_ref[:, j * fc:(j + 1) * fc]),
                _dot(xn, w_up_ref[:, D_FF + j * fc:D_FF + (j + 1) * fc]))

    n_chunks = D_FF // fc
    ups = up_proj(0)
    for j in range(n_chunks):
        nxt = up_proj(j + 1) if j + 1 < n_chunks else None
        vs = slice(j * fc, (j + 1) * fc)
        gs = slice(D_FF + j * fc, D_FF + (j + 1) * fc)
        upv, upg = ups
        half = fc // 2
        pieces = []
        for hh in range(2):
            pv = slice(vs.start + hh * half, vs.start + (hh + 1) * half)
            pg = slice(gs.start + hh * half, gs.start + (hh + 1) * half)
            loc = slice(hh * half, (hh + 1) * half)
            cv = _causal_conv(upv[:, loc], carry[:, pv], [conv_ref[k:k + 1, pv] for k in range(FFN_CONV)],
                              conv_ref[FFN_CONV:FFN_CONV + 1, pv])
            cg = _causal_conv(upg[:, loc], carry[:, pg], [conv_ref[k:k + 1, pg] for k in range(FFN_CONV)],
                              conv_ref[FFN_CONV:FFN_CONV + 1, pg])
            pieces.append((_gelu(cg) * cv).astype(_BF16))
        carry[:, vs] = upv[tm - (FFN_CONV - 1) * BATCH:]
        carry[:, gs] = upg[tm - (FFN_CONV - 1) * BATCH:]
        act = jnp.concatenate(pieces, axis=-1)
        down = _dot(act, w_down_ref[vs, :])
        if j + 1 < n_chunks:
            acc[...] += down
        ups = nxt
    total = acc[...] + down
    if last:
        out = _rmsnorm(total, gains_ref[1:2, :])
        for s in range(D_MODEL // LANES):
            hslab[s] = out[:, s * LANES:(s + 1) * LANES]
        _from_time_major(hslab, o_ref, FFN_TILE_STEPS)
    else:
        o_ref[...] = total


def _ffn_call(h, ya, yr, consts, l, batch_major_in, last):
    tm = FFN_TILE_ROWS
    out_shape = (BATCH, SEQ, D_MODEL) if last else (ROWS, D_MODEL)
    return pl.pallas_call(
        functools.partial(_ffn_kernel, batch_major_in=batch_major_in, last=last),
        grid=(ROWS // tm,),
        in_specs=[_stream_spec(batch_major_in, FFN_TILE_STEPS, D_MODEL),
                  pl.BlockSpec((tm, MIX_A_WIDTH), lambda i: (i, 0)),
                  _stream_spec(True, FFN_TILE_STEPS, RET_WIDTH)]
                 + [_layer_spec(c, l) for c in consts],
        out_specs=_stream_spec(last, FFN_TILE_STEPS, D_MODEL),
        out_shape=jax.ShapeDtypeStruct(out_shape, _F32),
        scratch_shapes=[pltpu.VMEM((tm, D_MODEL), _F32),
                        pltpu.VMEM(((FFN_CONV - 1) * BATCH, 2 * D_FF), _F32),
                        pltpu.VMEM((RET_WIDTH // LANES, tm, LANES), _F32),
                        pltpu.VMEM((D_MODEL // LANES, tm, LANES), _F32)],
        compiler_params=pltpu.CompilerParams(dimension_semantics=("arbitrary",),
                                             vmem_limit_bytes=VMEM_LIMIT_BYTES),
        name="outproj_ffn",
    )(h, ya, yr, *consts)


def _pack_gate(w):
    blk = LRU_WIDTH // LRU_BLOCKS
    per = MXU_WIDTH // blk
    w5 = w.astype(_F32).reshape(DEPTH, LRU_BLOCKS // per, per, blk, blk)
    t = jnp.einsum('lphij,hg->lphigj', w5, jnp.eye(per, dtype=_F32))
    return t.reshape(DEPTH, LRU_BLOCKS // per, MXU_WIDTH, MXU_WIDTH).astype(_BF16)


def _s5_params(lam_re, lam_im, log_dt, b_re, b_im, c_re, c_im):
    T = S5_BLOCK
    dt = jnp.exp(log_dt.astype(_F32))[..., None]
    lr, li = lam_re.astype(_F32), lam_im.astype(_F32)
    mag = jnp.exp(lr * dt)
    abar_re, abar_im = mag * jnp.cos(li * dt), mag * jnp.sin(li * dt)
    den = lr * lr + li * li
    nr, ni = abar_re - 1.0, abar_im
    coef_re = (nr * lr + ni * li) / den
    coef_im = (ni * lr - nr * li) / den
    bt_re = jnp.swapaxes(b_re.astype(_F32), -1, -2)
    bt_im = jnp.swapaxes(b_im.astype(_F32), -1, -2)
    bbar_re = coef_re[:, :, None, :] * bt_re - coef_im[:, :, None, :] * bt_im
    bbar_im = coef_re[:, :, None, :] * bt_im + coef_im[:, :, None, :] * bt_re
    cr, ci = c_re.astype(_F32), c_im.astype(_F32)

    def apow(e):
        e = e.astype(_F32)
        m = jnp.exp(lr[:, :, None, :] * dt[:, :, None, :] * e)
        ang = li[:, :, None, :] * dt[:, :, None, :] * e
        return m * jnp.cos(ang), m * jnp.sin(ang)

    gidx = jnp.arange(S5_GROUPS)
    lag_of = (jnp.arange(T)[None, :] - (gidx % (LANES // S5_GROUP))[:, None]) % T
    slot = (jnp.arange(2 * LANES) // S5_STATE) % 2
    own = (slot[None, :] == (gidx % 2)[:, None]).astype(_F32)[None, :, None, None, :]
    wide = lambda re, im: jnp.concatenate([re, re, im, im], axis=-1)

    def pair_rows(x):
        return x.reshape(DEPTH, S5_GROUPS // 2, 2 * T * S5_GROUP, 2 * LANES).astype(_BF16)

    def cmul_tiles(p_re, p_im, q_re, q_im, conj_im=False):
        sgn = -1.0 if conj_im else 1.0
        a1, a2 = wide(p_re, p_re)[:, :, :, None, :], wide(p_im, p_im)[:, :, :, None, :]
        b1, b2 = wide(q_re, sgn * q_im)[:, :, None, :, :], wide(-q_im, sgn * q_re)[:, :, None, :, :]
        return (a1 * b1 + a2 * b2) * own

    w1 = pair_rows(cmul_tiles(*apow((T - 1 - lag_of)[None, :, :, None]), bbar_re, bbar_im))
    w2t = pair_rows(cmul_tiles(*apow((lag_of + 1)[None, :, :, None]), cr, ci, conj_im=True))
    p_re, p_im = apow(jnp.arange(T)[None, None, :, None])
    x_re = p_re[:, :, :, None, :] * bbar_re[:, :, None] - p_im[:, :, :, None, :] * bbar_im[:, :, None]
    x_im = p_re[:, :, :, None, :] * bbar_im[:, :, None] + p_im[:, :, :, None, :] * bbar_re[:, :, None]
    kt = jnp.einsum('lgtcp,lgop->lgtco', x_re, cr) - jnp.einsum('lgtcp,lgop->lgtco', x_im, ci)
    expand = (jnp.arange(2 * LANES)[None, :] % S5_GROUP == jnp.arange(S5_GROUP)[:, None]).astype(_F32)
    kexp = jnp.einsum('lgtco,on->lgtcn', kt, expand)
    lag_col = jnp.repeat(jnp.concatenate([lag_of, lag_of], axis=1), S5_GROUP, axis=1)
    tau = lag_col[:, None, None, :] - lag_of[:, :, None, None]
    hit = (tau == jnp.arange(T)[None, None, :, None]).astype(_F32)
    own3 = (jnp.arange(2 * LANES) // LANES)[None, :] == (gidx % 2)[:, None]
    hit = hit * own3.astype(_F32)[:, None, None, :]
    w3 = pair_rows(jnp.sum(hit[None, :, :, :, None, :] * kexp[:, :, None, :, :, :], axis=3))

    a_re, a_im = apow(jnp.full((1, 1, 1, 1), T))
    a8 = jnp.concatenate([a_re.reshape(DEPTH, 1, S5_STATES), a_im.reshape(DEPTH, 1, S5_STATES)], axis=1)
    return a8, w1, w2t, w3


def kernel(x, positions, norm_mix, w_in, lru_conv_w, lru_conv_b, lru_wa, lru_ba, lru_wx, lru_bx, lru_lambda, lru_norm, s5_lambda_re, s5_lambda_im, s5_log_dt, s5_b_re, s5_b_im, s5_c_re, s5_c_im, s5_d, s5_w_glu, s5_b_glu, s5_norm, ret_norm, w_out, norm_ffn, w_up, ffn_conv_w, ffn_conv_b, w_down, norm_final):
    cos2, sin2 = _rotary_tables(positions)
    f32 = lambda v: v.astype(_F32)
    vec512 = jnp.concatenate(
        [f32(lru_conv_w),
         jnp.stack([f32(lru_conv_b), f32(lru_ba).reshape(DEPTH, LRU_WIDTH), f32(lru_bx).reshape(DEPTH, LRU_WIDTH),
                    jax.nn.softplus(-f32(lru_lambda)), f32(lru_norm), f32(s5_d), f32(s5_b_glu), f32(s5_norm)], axis=1),
         jnp.zeros((DEPTH, V_ROWS - V_S5_NORM - 1, LRU_WIDTH), _F32)], axis=1)
    a8, w1, w2, w3 = _s5_params(s5_lambda_re, s5_lambda_im, s5_log_dt, s5_b_re, s5_b_im, s5_c_re, s5_c_im)
    mix_consts = [f32(norm_mix).reshape(DEPTH, 1, D_MODEL), w_in.astype(_BF16), vec512,
                  _pack_gate(lru_wa), _pack_gate(lru_wx), a8, w1, w2, w3, s5_w_glu.astype(_BF16)]
    gains = jnp.stack([f32(norm_ffn), jnp.broadcast_to(f32(norm_final), (DEPTH, D_MODEL))], axis=1)
    ffn_conv = jnp.concatenate([f32(ffn_conv_w), f32(ffn_conv_b)[:, None, :]], axis=1)
    ffn_consts = [w_out.astype(_BF16), gains, w_up.astype(_BF16), ffn_conv, w_down.astype(_BF16)]
    gn_gain = f32(ret_norm).reshape(DEPTH, 1, RET_WIDTH)
    h = x
    for l in range(DEPTH):
        first, last = l == 0, l == DEPTH - 1
        ya, qkvg = _mix_call(h, mix_consts, l, batch_major_in=first)
        yr = _ret_call(qkvg, cos2, sin2, gn_gain, l)
        h = _ffn_call(h, ya, yr, ffn_consts, l, batch_major_in=first, last=last)
    return h
```
